```python
import math
import jax, jax.numpy as jnp
from jax import lax
import numpy as np

D_MODEL = 1024
BATCH = 1
SEQ = 16384
DEPTH = 4

N_MIXERS = 3
DEEPNORM_ALPHA = (2 * DEPTH) ** 0.25
DEEPNORM_BETA = (8 * DEPTH) ** -0.25
NORM_EPS = 1e-5

ATT_HEAD_DIM = 64
ATT_HEADS = D_MODEL // ATT_HEAD_DIM
DILATION_CONFIGS = ((128, 1), (512, 4), (2048, 16))
N_DIL_GROUPS = len(DILATION_CONFIGS)
ATT_BLOCK = 128
REL_BUCKETS = 32
REL_MAX_DIST = 2048

SSM_D_INNER = 2 * D_MODEL
SSM_HEAD_DIM = 64
SSM_HEADS = SSM_D_INNER // SSM_HEAD_DIM
SSM_GROUPS = 8
SSM_STATE = 128
SSM_CONV = 4
SSM_CHUNK = 128
SSM_CONV_DIM = SSM_D_INNER + 2 * SSM_GROUPS * SSM_STATE
SSM_IN_DIM = SSM_D_INNER + SSM_CONV_DIM + SSM_HEADS

HGRN_HEAD_DIM = 128
HGRN_HEADS = D_MODEL // HGRN_HEAD_DIM
HGRN_CHUNK = 64

MOE_GROUPS = 8
MOE_EXPERTS_PER_GROUP = 8
MOE_EXPERTS = MOE_GROUPS * MOE_EXPERTS_PER_GROUP
MOE_TOP_K = 2
MOE_D_FF = 512
MOE_BLOCK = 128

N_ATT_LAYERS = (DEPTH + 2) // 3
N_SSM_LAYERS = (DEPTH + 1) // 3
N_HGRN_LAYERS = DEPTH // 3

kernel_name = 'hybrid_dilated_ssd_hgrn2_hmoe'


def layer_norm(x, g, b):
    xf = x.astype(jnp.float32)
    xc = xf - jnp.mean(xf, -1, keepdims=True)
    var = jnp.mean(xc * xc, -1, keepdims=True)
    return xc * lax.rsqrt(var + NORM_EPS) * g + b


def rms_norm(x, w):
    xf = x.astype(jnp.float32)
    return xf * lax.rsqrt(jnp.mean(xf * xf, -1, keepdims=True) + NORM_EPS) * w


def t5_bucket(dist):
    n = jnp.maximum(dist, 0)
    max_exact = REL_BUCKETS // 2
    ratio = jnp.maximum(n, max_exact).astype(jnp.float32) / max_exact
    large = max_exact + (jnp.log(ratio) / math.log(REL_MAX_DIST / max_exact) * (REL_BUCKETS - max_exact)).astype(jnp.int32)
    large = jnp.minimum(large, REL_BUCKETS - 1)
    return jnp.where(n < max_exact, n, large)


def dilated_branch(q, k, v, table, window, dilation):
    b_, s_, h_, e_ = q.shape
    sub_len = s_ // dilation
    n_blk = -(-sub_len // ATT_BLOCK)
    pad = n_blk * ATT_BLOCK - sub_len

    def to_sub(t):
        t = t.reshape(b_, sub_len, dilation, h_, e_).transpose(0, 2, 1, 3, 4)
        t = jnp.pad(t, ((0, 0), (0, 0), (0, pad), (0, 0), (0, 0)))
        return t.reshape(b_, dilation, n_blk, ATT_BLOCK, h_, e_)

    def with_prev(t):
        prev = jnp.pad(t, ((0, 0), (0, 0), (1, 0), (0, 0), (0, 0), (0, 0)))[:, :, :-1]
        return jnp.concatenate([prev, t], axis=3)

    qs = to_sub(q)
    ks = with_prev(to_sub(k))
    vs = with_prev(to_sub(v))
    iq = jnp.arange(ATT_BLOCK)[:, None]
    ik = jnp.arange(2 * ATT_BLOCK)[None, :]
    rel = iq + ATT_BLOCK - ik
    band = (rel >= 0) & (rel <= window // dilation)
    bias = jnp.moveaxis(table[t5_bucket(rel * dilation)], -1, 0).astype(jnp.float32)
    first = jnp.arange(n_blk)[:, None, None] == 0
    valid = band[None] & ((~first) | (ik[None] >= ATT_BLOCK))
    scores = jnp.einsum('bdnqhe,bdnkhe->bdnhqk', qs, ks).astype(jnp.float32) * (ATT_HEAD_DIM ** -0.5) + bias
    scores = jnp.where(valid[:, None], scores, -jnp.inf)
    lse = jax.nn.logsumexp(scores, axis=-1)
    probs = jnp.exp(scores - lse[..., None]).astype(v.dtype)
    out = jnp.einsum('bdnhqk,bdnkhe->bdnqhe', probs, vs)
    out = out.reshape(b_, dilation, n_blk * ATT_BLOCK, h_, e_)[:, :, :sub_len]
    out = out.transpose(0, 2, 1, 3, 4).reshape(b_, s_, h_, e_)
    lse = jnp.swapaxes(lse, 3, 4).reshape(b_, dilation, n_blk * ATT_BLOCK, h_)[:, :, :sub_len]
    lse = lse.transpose(0, 2, 1, 3).reshape(b_, s_, h_)
    return out.astype(jnp.float32), lse


def dilated_attention(x, w_in, w_out, rel_bias):
    b_, s_, _ = x.shape
    qkv = (x @ w_in).reshape(b_, s_, N_DIL_GROUPS, 3, ATT_HEADS, ATT_HEAD_DIM)
    outs, lses = [], []
    for g, (window, dilation) in enumerate(DILATION_CONFIGS):
        o, l = dilated_branch(qkv[:, :, g, 0], qkv[:, :, g, 1], qkv[:, :, g, 2], rel_bias[:, g], window, dilation)
        outs.append(o)
        lses.append(l)
    wts = jax.nn.softmax(jnp.stack(lses, 0), axis=0)
    o = jnp.einsum('gbsh,gbshe->bshe', wts, jnp.stack(outs, 0))
    return o.reshape(b_, s_, ATT_HEADS * ATT_HEAD_DIM) @ w_out


def causal_depthwise_conv(u, w, b):
    u = u.astype(jnp.float32)
    y = lax.conv_general_dilated(u, w.astype(jnp.float32)[:, None, :], (1,), ((SSM_CONV - 1, 0),),
                                 dimension_numbers=('NWC', 'WIO', 'NWC'), feature_group_count=u.shape[-1])
    return y + b


def ssd_chunked(xdt, a, bm, cm):
    b_, s_, h_, p_ = xdt.shape
    g_ = bm.shape[2]
    r_ = h_ // g_
    n_c = s_ // SSM_CHUNK
    x = xdt.reshape(b_, n_c, SSM_CHUNK, g_, r_, p_)
    a = a.reshape(b_, n_c, SSM_CHUNK, g_, r_).transpose(0, 1, 3, 4, 2)
    bc = bm.reshape(b_, n_c, SSM_CHUNK, g_, -1)
    cc = cm.reshape(b_, n_c, SSM_CHUNK, g_, -1)
    a_cs = jnp.cumsum(a, axis=-1)
    causal = jnp.tril(jnp.ones((SSM_CHUNK, SSM_CHUNK), bool))
    seg = jnp.exp(jnp.where(causal, a_cs[..., :, None] - a_cs[..., None, :], -jnp.inf))
    cb = jnp.einsum('bclgn,bcsgn->bcgls', cc, bc)
    y_diag = jnp.einsum('bcgls,bcgrls,bcsgrp->bclgrp', cb, seg, x)
    decay_to_end = jnp.exp(a_cs[..., -1:] - a_cs)
    states = jnp.einsum('bcsgn,bcgrs,bcsgrp->bcgrpn', bc, decay_to_end, x)
    chunk_decay = jnp.exp(a_cs[..., -1])

    def carry_state(h, inp):
        st, dec = inp
        return h * dec[..., None, None] + st, h

    _, prev = lax.scan(carry_state, jnp.zeros_like(states[:, 0]),
                       (jnp.moveaxis(states, 1, 0), jnp.moveaxis(chunk_decay, 1, 0)))
    prev = jnp.moveaxis(prev, 0, 1)
    y_off = jnp.einsum('bclgn,bcgrpn,bcgrl->bclgrp', cc, prev, jnp.exp(a_cs))
    return (y_diag + y_off).reshape(b_, s_, h_, p_)


def gated_group_rms_norm(y, z, w):
    u = y * jax.nn.silu(z.astype(jnp.float32))
    shp = u.shape
    u = u.reshape(shp[:-1] + (SSM_GROUPS, shp[-1] // SSM_GROUPS))
    u = u * lax.rsqrt(jnp.mean(u * u, -1, keepdims=True) + NORM_EPS)
    return u.reshape(shp) * w


def mamba2_mixer(x, w_in, conv_w, conv_b, dt_bias, a_log, d_skip, norm_w, w_out):
    b_, s_, _ = x.shape
    proj = x @ w_in
    z = proj[..., :SSM_D_INNER]
    xbc = proj[..., SSM_D_INNER:SSM_D_INNER + SSM_CONV_DIM]
    dt = proj[..., SSM_D_INNER + SSM_CONV_DIM:]
    xbc = jax.nn.silu(causal_depthwise_conv(xbc, conv_w, conv_b))
    gn = SSM_GROUPS * SSM_STATE
    xs = xbc[..., :SSM_D_INNER].reshape(b_, s_, SSM_HEADS, SSM_HEAD_DIM)
    bm = xbc[..., SSM_D_INNER:SSM_D_INNER + gn].reshape(b_, s_, SSM_GROUPS, SSM_STATE)
    cm = xbc[..., SSM_D_INNER + gn:].reshape(b_, s_, SSM_GROUPS, SSM_STATE)
    dt = jax.nn.softplus(dt.astype(jnp.float32) + dt_bias)
    a = -jnp.exp(a_log.astype(jnp.float32))
    y = ssd_chunked(xs * dt[..., None], dt * a, bm, cm)
    y = y + xs * d_skip[:, None]
    y = gated_group_rms_norm(y.reshape(b_, s_, SSM_D_INNER), z, norm_w)
    return y @ w_out


def gla_chunked(q, k, v, log_f):
    b_, s_, h_, _ = q.shape
    n_c = s_ // HGRN_CHUNK

    def chunks(t):
        return t.reshape(b_, n_c, HGRN_CHUNK, h_, t.shape[-1]).transpose(1, 0, 3, 2, 4)

    qc, kc, vc = chunks(q), chunks(k), chunks(v)
    gc = jnp.cumsum(chunks(log_f), axis=3)
    mask = jnp.tril(jnp.ones((HGRN_CHUNK, HGRN_CHUNK), bool))[:, :, None]

    def step(state, inp):
        q_, k_, v_, g_ = inp
        decay = jnp.exp(jnp.where(mask, g_[:, :, :, None, :] - g_[:, :, None, :, :], -jnp.inf))
        attn = jnp.einsum('bhtk,bhsk,bhtsk->bhts', q_, k_, decay)
        o = jnp.einsum('bhts,bhsv->bhtv', attn, v_) + jnp.einsum('bhtk,bhkv->bhtv', q_ * jnp.exp(g_), state)
        g_last = g_[:, :, -1]
        k_dec = k_ * jnp.exp(g_last[:, :, None] - g_)
        state = state * jnp.exp(g_last)[..., None] + jnp.einsum('bhsk,bhsv->bhkv', k_dec, v_)
        return state, o

    init = jnp.zeros((b_, h_, q.shape[-1], v.shape[-1]), jnp.float32)
    _, o = lax.scan(step, init, (qc, kc, vc, gc))
    return o.transpose(1, 0, 3, 2, 4).reshape(b_, s_, h_, v.shape[-1])


def hgrn2_mixer(x, w_in, lower_bound, norm_w, w_out):
    b_, s_, _ = x.shape
    q, f, i, g = jnp.split(x @ w_in, 4, axis=-1)
    f = f.astype(jnp.float32)
    q = jax.nn.silu(q.astype(jnp.float32))
    k = (1.0 - lower_bound) * jax.nn.sigmoid(-f)
    log_f = jnp.logaddexp(jnp.log(lower_bound), jnp.log1p(-lower_bound) + jax.nn.log_sigmoid(f))

    def heads(t):
        return t.reshape(b_, s_, HGRN_HEADS, HGRN_HEAD_DIM)

    o = gla_chunked(heads(q), heads(k), heads(i.astype(jnp.float32)), heads(log_f))
    o = rms_norm(o, norm_w) * jax.nn.silu(heads(g).astype(jnp.float32))
    return o.reshape(b_, s_, D_MODEL) @ w_out


def routed_experts(x, expert_ids, gates, token_ids, w_gate, w_up, w_down):
    n_tok, d_ = x.shape
    n_asg = expert_ids.shape[0]
    order = jnp.argsort(expert_ids)
    e_sorted = expert_ids[order]
    counts = jnp.zeros((MOE_EXPERTS,), jnp.int32).at[expert_ids].add(1)
    padded = (counts + MOE_BLOCK - 1) // MOE_BLOCK * MOE_BLOCK
    start = jnp.cumsum(counts) - counts
    pend = jnp.cumsum(padded)
    pstart = pend - padded
    dest = pstart[e_sorted] + jnp.arange(n_asg) - start[e_sorted]
    n_blocks = (n_asg + MOE_EXPERTS * (MOE_BLOCK - 1) + MOE_BLOCK - 1) // MOE_BLOCK
    n_slots = n_blocks * MOE_BLOCK
    slot_token = jnp.full((n_slots,), n_tok, jnp.int32).at[dest].set(token_ids[order])
    slot_gate = jnp.zeros((n_slots,), gates.dtype).at[dest].set(gates[order])
    block_expert = jnp.minimum(jnp.searchsorted(pend, jnp.arange(n_blocks) * MOE_BLOCK, side='right'), MOE_EXPERTS - 1)
    x_pad = jnp.concatenate([x, jnp.zeros((1, d_), x.dtype)], axis=0)

    def run_block(args):
        e, tok, gt = args
        h = x_pad[tok]
        hidden = jax.nn.silu(h @ w_gate[e]) * (h @ w_up[e])
        return (hidden @ w_down[e]) * gt[:, None]

    y = lax.map(run_block, (block_expert, slot_token.reshape(n_blocks, MOE_BLOCK), slot_gate.reshape(n_blocks, MOE_BLOCK)))
    y = y.reshape(n_slots, d_)
    return jnp.zeros((n_tok + 1, d_), y.dtype).at[slot_token].add(y)[:n_tok]


def hier_moe(x, w_coarse, w_fine, w_gate, w_up, w_down):
    n_tok = x.shape[0]
    logits1 = (x @ w_coarse).astype(jnp.float32)
    p1 = jax.nn.softmax(logits1, axis=-1)
    grp = jnp.argmax(logits1, axis=-1)
    g1 = jnp.take_along_axis(p1, grp[:, None], axis=-1)[:, 0]
    logits2 = (x @ w_fine).astype(jnp.float32).reshape(n_tok, MOE_GROUPS, MOE_EXPERTS_PER_GROUP)
    logits2 = jnp.take_along_axis(logits2, grp[:, None, None], axis=1)[:, 0]
    top_l, top_i = lax.top_k(logits2, MOE_TOP_K)
    g2 = jax.nn.softmax(top_l, axis=-1)
    expert = grp[:, None] * MOE_EXPERTS_PER_GROUP + top_i
    gate = g1[:, None] * g2
    token_ids = jnp.repeat(jnp.arange(n_tok, dtype=jnp.int32), MOE_TOP_K)
    return routed_experts(x, expert.reshape(-1).astype(jnp.int32), gate.reshape(-1), token_ids, w_gate, w_up, w_down)


def setup_inputs(seed: int = 0) -> dict:
    key = jax.random.key(seed)
    ks = jax.random.split(key, 24)

    def nrm(k, shape, scale):
        return jax.random.normal(k, shape, jnp.float32) * scale

    att_w = ATT_HEADS * ATT_HEAD_DIM
    x = nrm(ks[0], (BATCH, SEQ, D_MODEL), 1.0)
    attn_w_in = nrm(ks[1], (N_ATT_LAYERS, D_MODEL, N_DIL_GROUPS * 3 * att_w), D_MODEL ** -0.5)
    attn_w_out = nrm(ks[2], (N_ATT_LAYERS, att_w, D_MODEL), att_w ** -0.5 * DEEPNORM_BETA)
    rel_bias = nrm(ks[3], (REL_BUCKETS, N_DIL_GROUPS, ATT_HEADS), 0.5)
    ssm_w_in = nrm(ks[4], (N_SSM_LAYERS, D_MODEL, SSM_IN_DIM), D_MODEL ** -0.5)
    ssm_conv_w = nrm(ks[5], (N_SSM_LAYERS, SSM_CONV, SSM_CONV_DIM), SSM_CONV ** -0.5)
    ssm_conv_b = nrm(ks[6], (N_SSM_LAYERS, SSM_CONV_DIM), 0.02)
    dt0 = jnp.exp(jax.random.uniform(ks[7], (N_SSM_LAYERS, SSM_HEADS), jnp.float32, math.log(1e-3), math.log(1e-1)))
    ssm_dt_bias = dt0 + jnp.log(-jnp.expm1(-dt0))
    ssm_a_log = jnp.log(jax.random.uniform(ks[8], (N_SSM_LAYERS, SSM_HEADS), jnp.float32, 1.0, 16.0))
    ssm_d = 1.0 + nrm(ks[9], (N_SSM_LAYERS, SSM_HEADS), 0.1)
    ssm_norm_w = 1.0 + nrm(ks[10], (N_SSM_LAYERS, SSM_D_INNER), 0.02)
    ssm_w_out = nrm(ks[11], (N_SSM_LAYERS, SSM_D_INNER, D_MODEL), SSM_D_INNER ** -0.5 * DEEPNORM_BETA)
    hgrn_w_in = nrm(ks[12], (N_HGRN_LAYERS, D_MODEL, 4 * D_MODEL), D_MODEL ** -0.5)
    hgrn_lower_bound = nrm(ks[13], (DEPTH, D_MODEL), 0.1)
    hgrn_norm_w = 1.0 + nrm(ks[14], (N_HGRN_LAYERS, HGRN_HEAD_DIM), 0.02)
    hgrn_w_out = nrm(ks[15], (N_HGRN_LAYERS, D_MODEL, D_MODEL), D_MODEL ** -0.5 * DEEPNORM_BETA)
    moe_w_coarse = nrm(ks[16], (DEPTH, D_MODEL, MOE_GROUPS), D_MODEL ** -0.5)
    moe_w_fine = nrm(ks[17], (DEPTH, D_MODEL, MOE_EXPERTS), D_MODEL ** -0.5)
    moe_w_gate = nrm(ks[18], (DEPTH, MOE_EXPERTS, D_MODEL, MOE_D_FF), D_MODEL ** -0.5)
    moe_w_up = nrm(ks[19], (DEPTH, MOE_EXPERTS, D_MODEL, MOE_D_FF), D_MODEL ** -0.5)
    moe_w_down = nrm(ks[20], (DEPTH, MOE_EXPERTS, MOE_D_FF, D_MODEL), MOE_D_FF ** -0.5 * DEEPNORM_BETA)
    ln_gamma = 1.0 + nrm(ks[21], (DEPTH, 2, D_MODEL), 0.02)
    ln_beta = nrm(ks[22], (DEPTH, 2, D_MODEL), 0.02)
    return {'x': x, 'attn_w_in': attn_w_in, 'attn_w_out': attn_w_out, 'rel_bias': rel_bias,
            'ssm_w_in': ssm_w_in, 'ssm_conv_w': ssm_conv_w, 'ssm_conv_b': ssm_conv_b, 'ssm_dt_bias': ssm_dt_bias,
            'ssm_a_log': ssm_a_log, 'ssm_d': ssm_d, 'ssm_norm_w': ssm_norm_w, 'ssm_w_out': ssm_w_out,
            'hgrn_w_in': hgrn_w_in, 'hgrn_lower_bound': hgrn_lower_bound, 'hgrn_norm_w': hgrn_norm_w,
            'hgrn_w_out': hgrn_w_out, 'moe_w_coarse': moe_w_coarse, 'moe_w_fine': moe_w_fine,
            'moe_w_gate': moe_w_gate, 'moe_w_up': moe_w_up, 'moe_w_down': moe_w_down,
            'ln_gamma': ln_gamma, 'ln_beta': ln_beta}


def reference(x, attn_w_in, attn_w_out, rel_bias, ssm_w_in, ssm_conv_w, ssm_conv_b, ssm_dt_bias, ssm_a_log,
              ssm_d, ssm_norm_w, ssm_w_out, hgrn_w_in, hgrn_lower_bound, hgrn_norm_w, hgrn_w_out,
              moe_w_coarse, moe_w_fine, moe_w_gate, moe_w_up, moe_w_down, ln_gamma, ln_beta):
    b_, s_, d_ = x.shape
    lbs = jax.nn.softmax(hgrn_lower_bound.astype(jnp.float32), axis=0)
    lbs = jnp.cumsum(lbs, axis=0) - lbs[0]
    h = x
    for layer in range(DEPTH):
        kind = layer % N_MIXERS
        j = layer // N_MIXERS
        if kind == 0:
            mix = dilated_attention(h, attn_w_in[j], attn_w_out[j], rel_bias)
        elif kind == 1:
            mix = mamba2_mixer(h, ssm_w_in[j], ssm_conv_w[j], ssm_conv_b[j], ssm_dt_bias[j], ssm_a_log[j],
                               ssm_d[j], ssm_norm_w[j], ssm_w_out[j])
        else:
            mix = hgrn2_mixer(h, hgrn_w_in[j], lbs[layer], hgrn_norm_w[j], hgrn_w_out[j])
        h = layer_norm(DEEPNORM_ALPHA * h + mix, ln_gamma[layer, 0], ln_beta[layer, 0])
        ffn = hier_moe(h.reshape(b_ * s_, d_), moe_w_coarse[layer], moe_w_fine[layer], moe_w_gate[layer],
                       moe_w_up[layer], moe_w_down[layer]).reshape(b_, s_, d_)
        h = layer_norm(DEEPNORM_ALPHA * h + ffn, ln_gamma[layer, 1], ln_beta[layer, 1])
    return h
```

```python
import functools
import math

import jax
import jax.numpy as jnp
from jax import lax
from jax.experimental import pallas as pl
from jax.experimental.pallas import tpu as pltpu

F32 = jnp.float32
BF16 = jnp.bfloat16
HIGHEST = lax.Precision.HIGHEST

D_MODEL = 1024
DEPTH = 4
N_MIXERS = 3
DEEPNORM_ALPHA = (2 * DEPTH) ** 0.25
NORM_EPS = 1e-5

ATT_HEAD_DIM = 64
ATT_HEADS = D_MODEL // ATT_HEAD_DIM
DILATION_CONFIGS = ((128, 1), (512, 4), (2048, 16))
N_DIL_GROUPS = len(DILATION_CONFIGS)
ATT_BLOCK = 128
REL_BUCKETS = 32
REL_MAX_DIST = 2048

SSM_D_INNER = 2 * D_MODEL
SSM_HEAD_DIM = 64
SSM_HEADS = SSM_D_INNER // SSM_HEAD_DIM
SSM_GROUPS = 8
SSM_HEADS_PER_GROUP = SSM_HEADS // SSM_GROUPS
SSM_STATE = 128
SSM_CONV = 4
SSM_CHUNK = 128
SSM_BC_DIM = 2 * SSM_GROUPS * SSM_STATE

HGRN_HEAD_DIM = 128
HGRN_HEADS = D_MODEL // HGRN_HEAD_DIM
GLA_CHUNK = 128
GLA_SUB = 8

MOE_GROUPS = 8
MOE_EXPERTS_PER_GROUP = 8
MOE_EXPERTS = MOE_GROUPS * MOE_EXPERTS_PER_GROUP
MOE_TOP_K = 2
MOE_D_FF = 512
MOE_BLOCK = 128

LANES = 128
SUBLANES = 8
VMEM_LIMIT_BYTES = 52 * 1024 * 1024


def _cparams(*sem):
    return pltpu.CompilerParams(dimension_semantics=sem, vmem_limit_bytes=VMEM_LIMIT_BYTES)


def _silu(x):
    return x / (1.0 + jnp.exp(-x))


def _floor_to_pow2(v, m):
    assert m & (m - 1) == 0
    return jnp.bitwise_and(v, -m)


def _mod_pow2(v, m):
    assert m & (m - 1) == 0
    return jnp.bitwise_and(v, m - 1)


def _mm_kernel(x_ref, w_ref, o_ref):
    o_ref[...] = jnp.dot(x_ref[...], w_ref[...], preferred_element_type=F32).astype(o_ref.dtype)


def _matmul(x, w, out_dtype, tm=1024, tn=1024):
    m, k = x.shape
    n = w.shape[1]
    tm = min(tm, m)
    tn = min(tn, n)
    assert m % tm == 0 and n % tn == 0
    return pl.pallas_call(
        _mm_kernel,
        grid=(n // tn, m // tm),
        in_specs=[pl.BlockSpec((tm, k), lambda j, i: (i, 0)),
                  pl.BlockSpec((k, tn), lambda j, i: (0, j))],
        out_specs=pl.BlockSpec((tm, tn), lambda j, i: (i, j)),
        out_shape=jax.ShapeDtypeStruct((m, n), out_dtype),
        compiler_params=_cparams("arbitrary", "arbitrary"),
        name="mm",
    )(x, w)


def _layer_norm(y, gamma, beta):
    mu = jnp.mean(y, axis=-1, keepdims=True)
    yc = y - mu
    var = jnp.mean(yc * yc, axis=-1, keepdims=True)
    return yc * lax.rsqrt(var + NORM_EPS) * gamma + beta


def _mix_ln_route_kernel(a_ref, w_ref, h_ref, gam_ref, bet_ref, wr_ref, hout_ref, route_ref):
    mix = jnp.dot(a_ref[...], w_ref[...], preferred_element_type=F32)
    hn = _layer_norm(DEEPNORM_ALPHA * h_ref[...] + mix, gam_ref[...], bet_ref[...])
    hout_ref[...] = hn
    logits = jnp.dot(hn, wr_ref[...], precision=HIGHEST, preferred_element_type=F32)
    lane = lax.broadcasted_iota(jnp.int32, logits.shape, 1).astype(F32)
    neg = -jnp.inf
    l1 = jnp.where(lane < MOE_GROUPS, logits, neg)
    m1 = jnp.max(l1, axis=-1, keepdims=True)
    grp = jnp.min(jnp.where(l1 == m1, lane, float(LANES)), axis=-1, keepdims=True)
    g1 = 1.0 / jnp.sum(jnp.exp(l1 - m1), axis=-1, keepdims=True)
    lo = MOE_GROUPS + grp * MOE_EXPERTS_PER_GROUP
    l2 = jnp.where(lane >= lo, jnp.where(lane < lo + MOE_EXPERTS_PER_GROUP, logits, neg), neg)
    t1 = jnp.max(l2, axis=-1, keepdims=True)
    i1 = jnp.min(jnp.where(l2 == t1, lane, float(LANES)), axis=-1, keepdims=True)
    l2b = jnp.where(lane == i1, neg, l2)
    t2 = jnp.max(l2b, axis=-1, keepdims=True)
    i2 = jnp.min(jnp.where(l2b == t2, lane, float(LANES)), axis=-1, keepdims=True)
    e21 = jnp.exp(t2 - t1)
    ga = g1 / (1.0 + e21)
    gb = g1 * e21 / (1.0 + e21)
    out = jnp.where(lane == 0, i1 - MOE_GROUPS,
                    jnp.where(lane == 1, i2 - MOE_GROUPS,
                              jnp.where(lane == 2, ga, jnp.where(lane == 3, gb, 0.0))))
    route_ref[...] = out


def _mix_ln_route(a, w, h, gamma, beta, w_route, tm=512):
    s, k = a.shape
    d = w.shape[1]
    return pl.pallas_call(
        _mix_ln_route_kernel,
        grid=(s // tm,),
        in_specs=[pl.BlockSpec((tm, k), lambda i: (i, 0)),
                  pl.BlockSpec((k, d), lambda i: (0, 0)),
                  pl.BlockSpec((tm, d), lambda i: (i, 0)),
                  pl.BlockSpec((1, d), lambda i: (0, 0)),
                  pl.BlockSpec((1, d), lambda i: (0, 0)),
                  pl.BlockSpec((d, LANES), lambda i: (0, 0))],
        out_specs=[pl.BlockSpec((tm, d), lambda i: (i, 0)),
                   pl.BlockSpec((tm, LANES), lambda i: (i, 0))],
        out_shape=[jax.ShapeDtypeStruct((s, d), F32),
                   jax.ShapeDtypeStruct((s, LANES), F32)],
        compiler_params=_cparams("arbitrary"),
        name="mix_ln_route",
    )(a, w, h, gamma, beta, w_route)


def _row_gather_copy(src_hbm, dst_vmem, sem, row, i):
    return pltpu.make_async_copy(src_hbm.at[pl.ds(row, 1), :], dst_vmem.at[pl.ds(i, 1), :], sem)


def _moe_kernel(bexp_ref, tok_ref, nused_ref, h_hbm, wg_ref, wu_ref, wd_ref, y_ref, xbuf, sems):
    b = pl.program_id(0)
    n_used = nused_ref[0]
    slot = lax.rem(b, 2)

    def start_gather(blk, buf_slot):
        def body(i, carry):
            tok = tok_ref[blk * MOE_BLOCK + i]
            _row_gather_copy(h_hbm, xbuf.at[buf_slot], sems.at[buf_slot], tok, i).start()
            return carry
        lax.fori_loop(0, MOE_BLOCK, body, 0)

    @pl.when(jnp.logical_and(b == 0, n_used > 0))
    def _():
        start_gather(0, 0)

    @pl.when(b + 1 < n_used)
    def _():
        start_gather(b + 1, 1 - slot)

    @pl.when(b < n_used)
    def _():
        pltpu.make_async_copy(h_hbm.at[pl.ds(0, MOE_BLOCK), :], xbuf.at[slot], sems.at[slot]).wait()
        x = xbuf[slot].astype(BF16)
        g = jnp.dot(x, wg_ref[0], preferred_element_type=F32)
        u = jnp.dot(x, wu_ref[0], preferred_element_type=F32)
        hid = (_silu(g) * u).astype(BF16)
        y_ref[...] = jnp.dot(hid, wd_ref[0], preferred_element_type=F32)

    @pl.when(b >= n_used)
    def _():
        y_ref[...] = jnp.zeros_like(y_ref)


def _moe_experts(h, block_expert, slot_token, n_used, w_gate, w_up, w_down):
    s, d = h.shape
    n_slots = slot_token.shape[0]
    n_blocks = n_slots // MOE_BLOCK
    dff = w_gate.shape[2]
    grid_spec = pltpu.PrefetchScalarGridSpec(
        num_scalar_prefetch=3,
        grid=(n_blocks,),
        in_specs=[pl.BlockSpec(memory_space=pl.ANY),
                  pl.BlockSpec((1, d, dff), lambda b, be, tok, nu: (be[b], 0, 0)),
                  pl.BlockSpec((1, d, dff), lambda b, be, tok, nu: (be[b], 0, 0)),
                  pl.BlockSpec((1, dff, d), lambda b, be, tok, nu: (be[b], 0, 0))],
        out_specs=pl.BlockSpec((MOE_BLOCK, d), lambda b, be, tok, nu: (b, 0)),
        scratch_shapes=[pltpu.VMEM((2, MOE_BLOCK, d), F32),
                        pltpu.SemaphoreType.DMA((2,))],
    )
    return pl.pallas_call(
        _moe_kernel,
        grid_spec=grid_spec,
        out_shape=jax.ShapeDtypeStruct((n_slots, d), F32),
        compiler_params=_cparams("arbitrary"),
        name="moe_experts",
    )(block_expert, slot_token, n_used, h, w_gate, w_up, w_down)


COMBINE_TILE = 256


def _combine_ln_kernel(pos_ref, y_hbm, h_ref, route_ref, gam_ref, bet_ref, hout_ref, hbf_ref, ybuf, sems):
    t = pl.program_id(0)
    n_t = pl.num_programs(0)
    slot = lax.rem(t, 2)

    def start_gather(tile, buf_slot):
        def body(i, carry):
            a = (tile * COMBINE_TILE + i) * MOE_TOP_K
            for k in range(MOE_TOP_K):
                _row_gather_copy(y_hbm, ybuf.at[buf_slot, k], sems.at[buf_slot], pos_ref[a + k], i).start()
            return carry
        lax.fori_loop(0, COMBINE_TILE, body, 0)

    @pl.when(t == 0)
    def _():
        start_gather(0, 0)

    @pl.when(t + 1 < n_t)
    def _():
        start_gather(t + 1, 1 - slot)

    for k in range(MOE_TOP_K):
        pltpu.make_async_copy(y_hbm.at[pl.ds(0, COMBINE_TILE), :], ybuf.at[slot, k], sems.at[slot]).wait()
    route = route_ref[...]
    ffn = route[:, 2:3] * ybuf[slot, 0] + route[:, 3:4] * ybuf[slot, 1]
    hn = _layer_norm(DEEPNORM_ALPHA * h_ref[...] + ffn, gam_ref[...], bet_ref[...])
    hout_ref[...] = hn
    hbf_ref[...] = hn.astype(BF16)


def _combine_ln(y_slots, pos, h, route, gamma, beta):
    s, d = h.shape
    tm = COMBINE_TILE
    grid_spec = pltpu.PrefetchScalarGridSpec(
        num_scalar_prefetch=1,
        grid=(s // tm,),
        in_specs=[pl.BlockSpec(memory_space=pl.ANY),
                  pl.BlockSpec((tm, d), lambda i, pos: (i, 0)),
                  pl.BlockSpec((tm, LANES), lambda i, pos: (i, 0)),
                  pl.BlockSpec((1, d), lambda i, pos: (0, 0)),
                  pl.BlockSpec((1, d), lambda i, pos: (0, 0))],
        out_specs=[pl.BlockSpec((tm, d), lambda i, pos: (i, 0)),
                   pl.BlockSpec((tm, d), lambda i, pos: (i, 0))],
        scratch_shapes=[pltpu.VMEM((2, MOE_TOP_K, tm, d), F32),
                        pltpu.SemaphoreType.DMA((2,))],
    )
    return pl.pallas_call(
        _combine_ln_kernel,
        grid_spec=grid_spec,
        out_shape=[jax.ShapeDtypeStruct((s, d), F32), jax.ShapeDtypeStruct((s, d), BF16)],
        compiler_params=_cparams("arbitrary"),
        name="combine_ln",
    )(pos, y_slots, h, route, gamma, beta)


def _dispatch_tables(route):
    s = route.shape[0]
    n_asg = s * MOE_TOP_K
    expert = route[:, :MOE_TOP_K].astype(jnp.int32).reshape(n_asg)
    onehot = (expert[:, None] == jnp.arange(MOE_EXPERTS, dtype=jnp.int32)[None, :]).astype(jnp.int32)
    csum = jnp.cumsum(onehot, axis=0)
    counts = csum[-1]
    padded = (counts + MOE_BLOCK - 1) // MOE_BLOCK * MOE_BLOCK
    pend = jnp.cumsum(padded)
    pstart = pend - padded
    pos = jnp.sum(onehot * (csum - 1 + pstart[None, :]), axis=1).astype(jnp.int32)
    n_blocks = (n_asg + MOE_EXPERTS * (MOE_BLOCK - 1) + MOE_BLOCK - 1) // MOE_BLOCK
    n_slots = n_blocks * MOE_BLOCK
    token_ids = jnp.arange(n_asg, dtype=jnp.int32) // MOE_TOP_K
    slot_token = jnp.zeros((n_slots,), jnp.int32).at[pos].set(token_ids)
    block_start = jnp.arange(n_blocks, dtype=jnp.int32) * MOE_BLOCK
    block_expert = jnp.minimum(jnp.searchsorted(pend, block_start, side='right'), MOE_EXPERTS - 1).astype(jnp.int32)
    n_used = (pend[-1:] // MOE_BLOCK).astype(jnp.int32)
    return pos, slot_token, block_expert, n_used


def _moe_layer(h, route, w_gate, w_up, w_down, gamma, beta):
    pos, slot_token, block_expert, n_used = _dispatch_tables(route)
    y_slots = _moe_experts(h, block_expert, slot_token, n_used, w_gate, w_up, w_down)
    return _combine_ln(y_slots, pos, h, route, gamma, beta)


def _t5_bucket(dist):
    n = jnp.maximum(dist, 0)
    max_exact = REL_BUCKETS // 2
    ratio = jnp.maximum(n, max_exact).astype(F32) / max_exact
    large = max_exact + (jnp.log(ratio) / math.log(REL_MAX_DIST / max_exact) * (REL_BUCKETS - max_exact)).astype(jnp.int32)
    large = jnp.minimum(large, REL_BUCKETS - 1)
    return jnp.where(n < max_exact, n, large)


def _bias_kernel(idx_ref, tab_ref, o_ref):
    idx = idx_ref[0]
    tab = tab_ref[0]
    acc = jnp.zeros(o_ref.shape[1:], F32)
    for b in range(REL_BUCKETS):
        acc = jnp.where(idx == b, tab[:, b:b + 1], acc)
    p = lax.broadcasted_iota(jnp.int32, acc.shape, 1)
    ik = _mod_pow2(p, 2 * ATT_BLOCK)
    iq = lax.shift_right_logical(p, int(math.log2(2 * ATT_BLOCK)))
    rel = iq + ATT_BLOCK - ik
    o_ref[0] = jnp.where((rel >= 0) & (rel <= ATT_BLOCK), acc, -jnp.inf)


def _attention_bias(rel_bias):
    iq = jnp.arange(ATT_BLOCK)[:, None]
    ik = jnp.arange(2 * ATT_BLOCK)[None, :]
    rel = iq + ATT_BLOCK - ik
    idx = jnp.stack([_t5_bucket(rel * dil) for _, dil in DILATION_CONFIGS], 0)
    n_pairs = ATT_BLOCK * 2 * ATT_BLOCK
    idx = idx.reshape(N_DIL_GROUPS, 1, n_pairs).astype(jnp.int32)
    tab = jnp.transpose(rel_bias.astype(F32), (1, 2, 0))
    bias = pl.pallas_call(
        _bias_kernel,
        grid=(N_DIL_GROUPS,),
        in_specs=[pl.BlockSpec((1, 1, n_pairs), lambda g: (g, 0, 0)),
                  pl.BlockSpec((1, ATT_HEADS, REL_BUCKETS), lambda g: (g, 0, 0))],
        out_specs=pl.BlockSpec((1, ATT_HEADS, n_pairs), lambda g: (g, 0, 0)),
        out_shape=jax.ShapeDtypeStruct((N_DIL_GROUPS, ATT_HEADS, n_pairs), F32),
        compiler_params=_cparams("arbitrary"),
        name="attn_bias",
    )(idx, tab)
    return bias.reshape(N_DIL_GROUPS, ATT_HEADS, ATT_BLOCK, 2 * ATT_BLOCK)


def _attn_kernel(q_ref, kp_ref, kc_ref, vp_ref, vc_ref, bias_ref, o_ref, lse_ref):
    n = pl.program_id(1)
    prev_mask = jnp.where(n > 0, 0.0, -jnp.inf)
    dn = (((1,), (1,)), ((), ()))
    for h in range(ATT_HEADS):
        hs = slice(h * ATT_HEAD_DIM, (h + 1) * ATT_HEAD_DIM)
        q = q_ref[:, hs] * (ATT_HEAD_DIM ** -0.5)
        bias = bias_ref[0, h]
        sp = lax.dot_general(q, kp_ref[:, hs], dn, preferred_element_type=F32) + bias[:, :ATT_BLOCK] + prev_mask
        sc = lax.dot_general(q, kc_ref[:, hs], dn, preferred_element_type=F32) + bias[:, ATT_BLOCK:]
        m = jnp.maximum(jnp.max(sp, axis=-1, keepdims=True), jnp.max(sc, axis=-1, keepdims=True))
        pp = jnp.exp(sp - m)
        pc = jnp.exp(sc - m)
        l = jnp.sum(pp, axis=-1, keepdims=True) + jnp.sum(pc, axis=-1, keepdims=True)
        o = (jnp.dot(pp.astype(BF16), vp_ref[:, hs], preferred_element_type=F32)
             + jnp.dot(pc.astype(BF16), vc_ref[:, hs], preferred_element_type=F32))
        o_ref[:, hs] = (o / l).astype(o_ref.dtype)
        lse_ref[:, hs] = jnp.broadcast_to(m + jnp.log(l), (ATT_BLOCK, ATT_HEAD_DIM))


def _dilated_branch(qkv, bias, g, dilation):
    s = qkv.shape[0]
    hd = ATT_HEADS * ATT_HEAD_DIM
    sub_len = s // dilation
    n_blk = sub_len // ATT_BLOCK
    assert s % dilation == 0 and sub_len % ATT_BLOCK == 0
    per_tok = qkv.shape[1] // hd
    view = qkv.reshape(sub_len, dilation * qkv.shape[1])

    def col(r, j):
        return r * per_tok + g * 3 + j

    def cur(j):
        return pl.BlockSpec((ATT_BLOCK, hd), lambda r, n: (n, col(r, j)))

    def prev(j):
        return pl.BlockSpec((ATT_BLOCK, hd), lambda r, n: (jnp.maximum(n - 1, 0), col(r, j)))

    out, lse = pl.pallas_call(
        _attn_kernel,
        grid=(dilation, n_blk),
        in_specs=[cur(0), prev(1), cur(1), prev(2), cur(2),
                  pl.BlockSpec((1, ATT_HEADS, ATT_BLOCK, 2 * ATT_BLOCK), lambda r, n: (g, 0, 0, 0))],
        out_specs=[pl.BlockSpec((ATT_BLOCK, hd), lambda r, n: (n, r)),
                   pl.BlockSpec((ATT_BLOCK, hd), lambda r, n: (n, r))],
        out_shape=[jax.ShapeDtypeStruct((sub_len, dilation * hd), BF16),
                   jax.ShapeDtypeStruct((sub_len, dilation * hd), F32)],
        compiler_params=_cparams("arbitrary", "arbitrary"),
        name=f"dilated_attn_g{g}",
    )(view, view, view, view, view, bias)
    return out.reshape(s, hd), lse.reshape(s, hd)


def _attn_combine_kernel(o0, o1, o2, l0, l1, l2, out_ref):
    a0, a1, a2 = l0[...], l1[...], l2[...]
    m = jnp.maximum(jnp.maximum(a0, a1), a2)
    e0, e1, e2 = jnp.exp(a0 - m), jnp.exp(a1 - m), jnp.exp(a2 - m)
    num = e0 * o0[...].astype(F32) + e1 * o1[...].astype(F32) + e2 * o2[...].astype(F32)
    out_ref[...] = (num / (e0 + e1 + e2)).astype(out_ref.dtype)


def _attn_combine(outs, lses, tm=512):
    s, d = outs[0].shape
    spec = pl.BlockSpec((tm, d), lambda i: (i, 0))
    return pl.pallas_call(
        _attn_combine_kernel,
        grid=(s // tm,),
        in_specs=[spec] * 6,
        out_specs=spec,
        out_shape=jax.ShapeDtypeStruct((s, d), BF16),
        compiler_params=_cparams("arbitrary"),
        name="attn_combine",
    )(*outs, *lses)


def _dilated_attention(h_bf, w_in_bf, bias):
    qkv = _matmul(h_bf, w_in_bf, BF16)
    outs, lses = [], []
    for g, (_, dilation) in enumerate(DILATION_CONFIGS):
        o, l = _dilated_branch(qkv, bias, g, dilation)
        outs.append(o)
        lses.append(l)
    return _attn_combine(outs, lses)


def _head_expand(n_heads, width):
    r = lax.broadcasted_iota(jnp.int32, (LANES, n_heads * width), 0)
    c = lax.broadcasted_iota(jnp.int32, (LANES, n_heads * width), 1)
    return (_floor_to_pow2(c, width) == r * width).astype(F32)


def _causal_conv_silu(cur, halo, w, b):
    rows = lax.broadcasted_iota(jnp.int32, (SUBLANES, cur.shape[1]), 0)
    acc = cur * w[SSM_CONV - 1:SSM_CONV, :] + b
    for j in range(1, SSM_CONV):
        shifted = pltpu.roll(cur, j, 0)
        head = jnp.where(rows < j, pltpu.roll(halo, j, 0), shifted[:SUBLANES])
        shifted = jnp.concatenate([head, shifted[SUBLANES:]], axis=0)
        acc = acc + shifted * w[SSM_CONV - 1 - j:SSM_CONV - j, :]
    return _silu(acc)


def _ssd_kernel(z_ref, x_ref, bc_ref, xh_ref, bch_ref, dt_ref, cwx_ref, cbx_ref, cwbc_ref, cbbc_ref,
                dtb_ref, aneg_ref, dskip_ref, nw_ref, o_ref, state_ref):
    c = pl.program_id(0)

    @pl.when(c == 0)
    def _():
        state_ref[...] = jnp.zeros_like(state_ref)

    first = c == 0
    xh = jnp.where(first, 0.0, xh_ref[...].astype(F32))
    bch = jnp.where(first, 0.0, bch_ref[...].astype(F32))
    xs = _causal_conv_silu(x_ref[...].astype(F32), xh, cwx_ref[...], cbx_ref[...])
    bc = _causal_conv_silu(bc_ref[...].astype(F32), bch, cwbc_ref[...], cbbc_ref[...])
    gn = SSM_GROUPS * SSM_STATE

    x_dt = dt_ref[...] + dtb_ref[...]
    dt = jnp.maximum(x_dt, 0.0) + jnp.log1p(jnp.exp(-jnp.abs(x_dt)))
    a = dt * aneg_ref[...]
    row = lax.broadcasted_iota(jnp.int32, (SSM_CHUNK, SSM_CHUNK), 0)
    colv = lax.broadcasted_iota(jnp.int32, (SSM_CHUNK, SSM_CHUNK), 1)
    causal = colv <= row
    a_cs = jnp.dot(causal.astype(F32), a, precision=HIGHEST, preferred_element_type=F32)
    a_cs_t = a_cs.T
    expand = _head_expand(SSM_HEADS, SSM_HEAD_DIM)
    a_last = a_cs[SSM_CHUNK - 1:SSM_CHUNK, :]
    dt_e = jnp.dot(dt, expand, precision=HIGHEST, preferred_element_type=F32)
    in_decay_e = jnp.dot(jnp.exp(a_cs), expand, precision=HIGHEST, preferred_element_type=F32)
    out_decay_e = jnp.dot(jnp.exp(a_last - a_cs), expand, precision=HIGHEST, preferred_element_type=F32)
    chunk_decay_e = in_decay_e[SSM_CHUNK - 1:SSM_CHUNK, :]

    xdt = xs * dt_e
    xdt_bf = xdt.astype(BF16)
    xend_bf = (xdt * out_decay_e).astype(BF16)
    gw = SSM_HEADS_PER_GROUP * SSM_HEAD_DIM
    ys = []
    for g in range(SSM_GROUPS):
        b_g = bc[:, g * SSM_STATE:(g + 1) * SSM_STATE]
        c_g = bc[:, gn + g * SSM_STATE:gn + (g + 1) * SSM_STATE].astype(BF16)
        cb = lax.dot_general(c_g, b_g.astype(BF16), (((1,), (1,)), ((), ())), preferred_element_type=F32)
        prev = state_ref[g]
        y_off = jnp.dot(c_g, prev.astype(BF16), preferred_element_type=F32) * in_decay_e[:, g * gw:(g + 1) * gw]
        y_parts = []
        for r in range(SSM_HEADS_PER_GROUP):
            hh = g * SSM_HEADS_PER_GROUP + r
            seg = jnp.exp(jnp.where(causal, a_cs[:, hh:hh + 1] - a_cs_t[hh:hh + 1, :], -jnp.inf))
            y_parts.append(jnp.dot((cb * seg).astype(BF16), xdt_bf[:, hh * SSM_HEAD_DIM:(hh + 1) * SSM_HEAD_DIM],
                                   preferred_element_type=F32))
        ys.append(jnp.concatenate(y_parts, axis=1) + y_off)
        new = jnp.dot(b_g.T.astype(BF16), xend_bf[:, g * gw:(g + 1) * gw], preferred_element_type=F32)
        state_ref[g] = prev * chunk_decay_e[:, g * gw:(g + 1) * gw] + new
    y = jnp.concatenate(ys, axis=1) + xs * dskip_ref[...]
    u = y * _silu(z_ref[...].astype(F32))
    gsz = SSM_D_INNER // SSM_GROUPS
    outs = []
    for g in range(SSM_GROUPS):
        ug = u[:, g * gsz:(g + 1) * gsz]
        outs.append(ug * lax.rsqrt(jnp.mean(ug * ug, axis=-1, keepdims=True) + NORM_EPS))
    o_ref[...] = (jnp.concatenate(outs, axis=1) * nw_ref[...]).astype(o_ref.dtype)


def _mamba2_inner(zxbc, dt_raw, conv_w, conv_b, dt_bias, a_log, d_skip, norm_w):
    s = zxbc.shape[0]
    n_c = s // SSM_CHUNK
    di = SSM_D_INNER
    halo_blocks = SSM_CHUNK // SUBLANES

    def pad_lanes(v):
        return jnp.pad(v.astype(F32), (0, LANES - v.shape[0])).reshape(1, LANES)

    cw = conv_w.astype(F32)
    cb = conv_b.astype(F32).reshape(1, -1)
    dskip = jnp.repeat(d_skip.astype(F32), SSM_HEAD_DIM).reshape(1, di)
    a_neg = pad_lanes(-jnp.exp(a_log.astype(F32)))

    def halo(colblk):
        return pl.BlockSpec((SUBLANES, di), lambda c: (jnp.maximum(c * halo_blocks - 1, 0), colblk))

    full = lambda shape: pl.BlockSpec(shape, lambda c: (0, 0))
    return pl.pallas_call(
        _ssd_kernel,
        grid=(n_c,),
        in_specs=[pl.BlockSpec((SSM_CHUNK, di), lambda c: (c, 0)),
                  pl.BlockSpec((SSM_CHUNK, di), lambda c: (c, 1)),
                  pl.BlockSpec((SSM_CHUNK, SSM_BC_DIM), lambda c: (c, 2)),
                  halo(1), halo(2),
                  pl.BlockSpec((SSM_CHUNK, LANES), lambda c: (c, 0)),
                  full((SSM_CONV, di)), full((1, di)), full((SSM_CONV, SSM_BC_DIM)), full((1, SSM_BC_DIM)),
                  full((1, LANES)), full((1, LANES)), full((1, di)), full((1, di))],
        out_specs=pl.BlockSpec((SSM_CHUNK, di), lambda c: (c, 0)),
        out_shape=jax.ShapeDtypeStruct((s, di), BF16),
        scratch_shapes=[pltpu.VMEM((SSM_GROUPS, SSM_STATE, SSM_HEADS_PER_GROUP * SSM_HEAD_DIM), F32)],
        compiler_params=_cparams("arbitrary"),
        name="ssd_chunk_scan",
    )(zxbc, zxbc, zxbc, zxbc, zxbc, dt_raw, cw[:, :di], cb[:, :di], cw[:, di:], cb[:, di:],
      pad_lanes(dt_bias), a_neg, dskip, norm_w.astype(F32).reshape(1, di))


def _mamba2_mixer(h_bf, w_in, conv_w, conv_b, dt_bias, a_log, d_skip, norm_w):
    n_main = SSM_D_INNER + SSM_D_INNER + SSM_BC_DIM
    assert SSM_BC_DIM == SSM_D_INNER
    w_main = w_in[:, :n_main].astype(BF16)
    w_dt = jnp.pad(w_in[:, n_main:], ((0, 0), (0, LANES - SSM_HEADS))).astype(BF16)
    zxbc = _matmul(h_bf, w_main, BF16)
    dt_raw = _matmul(h_bf, w_dt, F32)
    return _mamba2_inner(zxbc, dt_raw, conv_w, conv_b, dt_bias, a_log, d_skip, norm_w)


def _gla_kernel(q_ref, f_ref, v_ref, gate_ref, lb_ref, nw_ref, o_ref, state_ref):
    c = pl.program_id(0)

    @pl.when(c == 0)
    def _():
        state_ref[...] = jnp.zeros_like(state_ref)

    L = GLA_CHUNK
    K = HGRN_HEAD_DIM
    n_sub = L // GLA_SUB
    row = lax.broadcasted_iota(jnp.int32, (L, L), 0)
    col = lax.broadcasted_iota(jnp.int32, (L, L), 1)
    tri = (col <= row).astype(F32)
    sub3 = lax.broadcasted_iota(jnp.int32, (n_sub, GLA_SUB, K), 1)
    lane3 = lax.broadcasted_iota(jnp.int32, (n_sub, GLA_SUB, L), 2)
    base3 = lax.broadcasted_iota(jnp.int32, (n_sub, GLA_SUB, L), 0) * GLA_SUB
    rowk = lax.broadcasted_iota(jnp.int32, (L, K), 0)
    levels = []
    m = 2 * GLA_SUB
    while m <= L:
        half = m // 2
        second = _mod_pow2(rowk, m) >= half
        sel = (col == _floor_to_pow2(row, m) + half - 1).astype(F32)
        same = _floor_to_pow2(row, m) == _floor_to_pow2(col, m)
        levels.append((second, sel, same))
        m *= 2

    for h in range(HGRN_HEADS):
        hs = slice(h * K, (h + 1) * K)
        f = f_ref[:, hs]
        lb = lb_ref[:, hs]
        q = _silu(q_ref[:, hs].astype(F32))
        v_bf = v_ref[:, hs]
        l1p = jnp.log1p(jnp.exp(-jnp.abs(f)))
        k = (1.0 - lb) * jnp.exp(jnp.minimum(-f, 0.0) - l1p)
        ta = jnp.log(lb)
        tb = jnp.log1p(-lb) + jnp.minimum(f, 0.0) - l1p
        log_f = jnp.maximum(ta, tb) + jnp.log1p(jnp.exp(-jnp.abs(ta - tb)))
        g = jnp.dot(tri, log_f, precision=HIGHEST, preferred_element_type=F32)

        g3 = g.reshape(n_sub, GLA_SUB, K)
        q3 = q.reshape(n_sub, GLA_SUB, K)
        k3 = k.reshape(n_sub, GLA_SUB, K)
        a3 = jnp.zeros((n_sub, GLA_SUB, L), F32)
        for s in range(GLA_SUB):
            e = jnp.exp(jnp.where(sub3 >= s, g3 - g3[:, s:s + 1, :], -jnp.inf))
            val = jnp.sum(q3 * e * k3[:, s:s + 1, :], axis=-1, keepdims=True)
            a3 = a3 + jnp.where(lane3 == base3 + s, val, 0.0)
        att = a3.reshape(L, L)

        for second, sel, same in levels:
            gb = jnp.dot(sel, g, precision=HIGHEST, preferred_element_type=F32)
            ql = (q * jnp.exp(jnp.where(second, g - gb, -jnp.inf))).astype(BF16)
            kl = (k * jnp.exp(jnp.where(second, -jnp.inf, gb - g))).astype(BF16)
            al = lax.dot_general(ql, kl, (((1,), (1,)), ((), ())), preferred_element_type=F32)
            att = att + jnp.where(same, al, 0.0)

        state_t = state_ref[h]
        o = (jnp.dot(att.astype(BF16), v_bf, preferred_element_type=F32)
             + lax.dot_general((q * jnp.exp(g)).astype(BF16), state_t.astype(BF16), (((1,), (1,)), ((), ())),
                               preferred_element_type=F32))
        g_last = g[L - 1:L, :]
        k_dec = (k * jnp.exp(g_last - g)).astype(BF16)
        v_t = v_bf.astype(F32).T.astype(BF16)
        state_ref[h] = state_t * jnp.exp(g_last) + jnp.dot(v_t, k_dec, preferred_element_type=F32)
        o = o * lax.rsqrt(jnp.mean(o * o, axis=-1, keepdims=True) + NORM_EPS) * nw_ref[...]
        o_ref[:, hs] = (o * _silu(gate_ref[:, hs].astype(F32))).astype(o_ref.dtype)


def _gla(qig, f, lb, norm_w):
    s, d = f.shape
    n_c = s // GLA_CHUNK
    return pl.pallas_call(
        _gla_kernel,
        grid=(n_c,),
        in_specs=[pl.BlockSpec((GLA_CHUNK, d), lambda c: (c, 0)),
                  pl.BlockSpec((GLA_CHUNK, d), lambda c: (c, 0)),
                  pl.BlockSpec((GLA_CHUNK, d), lambda c: (c, 1)),
                  pl.BlockSpec((GLA_CHUNK, d), lambda c: (c, 2)),
                  pl.BlockSpec((1, d), lambda c: (0, 0)),
                  pl.BlockSpec((1, HGRN_HEAD_DIM), lambda c: (0, 0))],
        out_specs=pl.BlockSpec((GLA_CHUNK, d), lambda c: (c, 0)),
        out_shape=jax.ShapeDtypeStruct((s, d), BF16),
        scratch_shapes=[pltpu.VMEM((HGRN_HEADS, HGRN_HEAD_DIM, HGRN_HEAD_DIM), F32)],
        compiler_params=_cparams("arbitrary"),
        name="gla_chunk_scan",
    )(qig, f, qig, qig, lb, norm_w)


def _hgrn2_mixer(h_bf, w_in, lb, norm_w):
    d = D_MODEL
    w_q, w_f, w_i, w_g = (w_in[:, j * d:(j + 1) * d] for j in range(4))
    qig = _matmul(h_bf, jnp.concatenate([w_q, w_i, w_g], axis=1).astype(BF16), BF16)
    f = _matmul(h_bf, w_f.astype(BF16), F32)
    return _gla(qig, f, lb.astype(F32).reshape(1, d), norm_w.astype(F32).reshape(1, HGRN_HEAD_DIM))


def kernel(x, attn_w_in, attn_w_out, rel_bias, ssm_w_in, ssm_conv_w, ssm_conv_b, ssm_dt_bias, ssm_a_log, ssm_d, ssm_norm_w, ssm_w_out, hgrn_w_in, hgrn_lower_bound, hgrn_norm_w, hgrn_w_out, moe_w_coarse, moe_w_fine, moe_w_gate, moe_w_up, moe_w_down, ln_gamma, ln_beta):
    b_, s_, d_ = x.shape
    assert b_ == 1 and d_ == D_MODEL
    lbs = jax.nn.softmax(hgrn_lower_bound.astype(F32), axis=0)
    lbs = jnp.cumsum(lbs, axis=0) - lbs[0]
    bias = _attention_bias(rel_bias)
    h = x.reshape(s_, d_).astype(F32)
    h_bf = h.astype(BF16)
    for layer in range(DEPTH):
        kind = layer % N_MIXERS
        j = layer // N_MIXERS
        if kind == 0:
            a = _dilated_attention(h_bf, attn_w_in[j].astype(BF16), bias)
            w_out = attn_w_out[j]
        elif kind == 1:
            a = _mamba2_mixer(h_bf, ssm_w_in[j], ssm_conv_w[j], ssm_conv_b[j], ssm_dt_bias[j], ssm_a_log[j],
                              ssm_d[j], ssm_norm_w[j])
            w_out = ssm_w_out[j]
        else:
            a = _hgrn2_mixer(h_bf, hgrn_w_in[j], lbs[layer], hgrn_norm_w[j])
            w_out = hgrn_w_out[j]
        gam = ln_gamma[layer].astype(F32)
        bet = ln_beta[layer].astype(F32)
        w_route = jnp.pad(jnp.concatenate([moe_w_coarse[layer], moe_w_fine[layer]], axis=1).astype(F32),
                          ((0, 0), (0, LANES - MOE_GROUPS - MOE_EXPERTS)))
        h, route = _mix_ln_route(a, w_out.astype(BF16), h, gam[0:1], bet[0:1], w_route)
        h, h_bf = _moe_layer(h, route, moe_w_gate[layer].astype(BF16), moe_w_up[layer].astype(BF16),
                             moe_w_down[layer].astype(BF16), gam[1:2], bet[1:2])
    return h.reshape(b_, s_, d_)
```

```python
import functools
import math

import jax
import jax.numpy as jnp
from jax import lax
from jax.experimental import pallas as pl
from jax.experimental.pallas import tpu as pltpu

F32 = jnp.float32
BF16 = jnp.bfloat16
HIGHEST = lax.Precision.HIGHEST

D_MODEL = 1024
DEPTH = 4
N_MIXERS = 3
DEEPNORM_ALPHA = (2 * DEPTH) ** 0.25
NORM_EPS = 1e-5

ATT_HEAD_DIM = 64
ATT_HEADS = D_MODEL // ATT_HEAD_DIM
DILATION_CONFIGS = ((128, 1), (512, 4), (2048, 16))
N_DIL_GROUPS = len(DILATION_CONFIGS)
ATT_BLOCK = 128
REL_BUCKETS = 32
REL_MAX_DIST = 2048

SSM_D_INNER = 2 * D_MODEL
SSM_HEAD_DIM = 64
SSM_HEADS = SSM_D_INNER // SSM_HEAD_DIM
SSM_GROUPS = 8
SSM_HEADS_PER_GROUP = SSM_HEADS // SSM_GROUPS
SSM_STATE = 128
SSM_CONV = 4
SSM_CHUNK = 128
SSM_BC_DIM = 2 * SSM_GROUPS * SSM_STATE

HGRN_HEAD_DIM = 128
HGRN_HEADS = D_MODEL // HGRN_HEAD_DIM
GLA_CHUNK = 128
GLA_SUB = 8

MOE_GROUPS = 8
MOE_EXPERTS_PER_GROUP = 8
MOE_EXPERTS = MOE_GROUPS * MOE_EXPERTS_PER_GROUP
MOE_TOP_K = 2
MOE_D_FF = 512
MOE_BLOCK = 128

LANES = 128
SUBLANES = 8
VMEM_LIMIT_BYTES = 52 * 1024 * 1024


def _cparams(*sem):
    return pltpu.CompilerParams(dimension_semantics=sem, vmem_limit_bytes=VMEM_LIMIT_BYTES)


def _silu(x):
    return x / (1.0 + jnp.exp(-x))


def _floor_to_pow2(v, m):
    assert m & (m - 1) == 0
    return jnp.bitwise_and(v, -m)


def _mod_pow2(v, m):
    assert m & (m - 1) == 0
    return jnp.bitwise_and(v, m - 1)


def _mm_kernel(x_ref, w_ref, o_ref):
    o_ref[...] = jnp.dot(x_ref[...], w_ref[...], preferred_element_type=F32).astype(o_ref.dtype)


def _matmul(x, w, out_dtype, tm=1024, tn=1024):
    m, k = x.shape
    n = w.shape[1]
    tm = min(tm, m)
    tn = min(tn, n)
    assert m % tm == 0 and n % tn == 0
    return pl.pallas_call(
        _mm_kernel,
        grid=(n // tn, m // tm),
        in_specs=[pl.BlockSpec((tm, k), lambda j, i: (i, 0)),
                  pl.BlockSpec((k, tn), lambda j, i: (0, j))],
        out_specs=pl.BlockSpec((tm, tn), lambda j, i: (i, j)),
        out_shape=jax.ShapeDtypeStruct((m, n), out_dtype),
        compiler_params=_cparams("arbitrary", "arbitrary"),
        name="mm",
    )(x, w)


def _layer_norm(y, gamma, beta):
    mu = jnp.mean(y, axis=-1, keepdims=True)
    yc = y - mu
    var = jnp.mean(yc * yc, axis=-1, keepdims=True)
    return yc * lax.rsqrt(var + NORM_EPS) * gamma + beta


def _mix_ln_route_kernel(a_ref, w_ref, h_ref, gam_ref, bet_ref, wr_ref, hout_ref, route_ref):
    mix = jnp.dot(a_ref[...], w_ref[...], preferred_element_type=F32)
    hn = _layer_norm(DEEPNORM_ALPHA * h_ref[...] + mix, gam_ref[...], bet_ref[...])
    hout_ref[...] = hn
    logits = jnp.dot(hn, wr_ref[...], precision=HIGHEST, preferred_element_type=F32)
    lane = lax.broadcasted_iota(jnp.int32, logits.shape, 1).astype(F32)
    neg = -jnp.inf
    l1 = jnp.where(lane < MOE_GROUPS, logits, neg)
    m1 = jnp.max(l1, axis=-1, keepdims=True)
    grp = jnp.min(jnp.where(l1 == m1, lane, float(LANES)), axis=-1, keepdims=True)
    g1 = 1.0 / jnp.sum(jnp.exp(l1 - m1), axis=-1, keepdims=True)
    lo = MOE_GROUPS + grp * MOE_EXPERTS_PER_GROUP
    l2 = jnp.where(lane >= lo, jnp.where(lane < lo + MOE_EXPERTS_PER_GROUP, logits, neg), neg)
    t1 = jnp.max(l2, axis=-1, keepdims=True)
    i1 = jnp.min(jnp.where(l2 == t1, lane, float(LANES)), axis=-1, keepdims=True)
    l2b = jnp.where(lane == i1, neg, l2)
    t2 = jnp.max(l2b, axis=-1, keepdims=True)
    i2 = jnp.min(jnp.where(l2b == t2, lane, float(LANES)), axis=-1, keepdims=True)
    e21 = jnp.exp(t2 - t1)
    ga = g1 / (1.0 + e21)
    gb = g1 * e21 / (1.0 + e21)
    out = jnp.where(lane == 0, i1 - MOE_GROUPS,
                    jnp.where(lane == 1, i2 - MOE_GROUPS,
                              jnp.where(lane == 2, ga, jnp.where(lane == 3, gb, 0.0))))
    route_ref[...] = out


def _mix_ln_route(a, w, h, gamma, beta, w_route, tm=512):
    s, k = a.shape
    d = w.shape[1]
    return pl.pallas_call(
        _mix_ln_route_kernel,
        grid=(s // tm,),
        in_specs=[pl.BlockSpec((tm, k), lambda i: (i, 0)),
                  pl.BlockSpec((k, d), lambda i: (0, 0)),
                  pl.BlockSpec((tm, d), lambda i: (i, 0)),
                  pl.BlockSpec((1, d), lambda i: (0, 0)),
                  pl.BlockSpec((1, d), lambda i: (0, 0)),
                  pl.BlockSpec((d, LANES), lambda i: (0, 0))],
        out_specs=[pl.BlockSpec((tm, d), lambda i: (i, 0)),
                   pl.BlockSpec((tm, LANES), lambda i: (i, 0))],
        out_shape=[jax.ShapeDtypeStruct((s, d), F32),
                   jax.ShapeDtypeStruct((s, LANES), F32)],
        compiler_params=_cparams("arbitrary"),
        name="mix_ln_route",
    )(a, w, h, gamma, beta, w_route)


def _row_gather_copy(src_hbm, dst_vmem, sem, row, i):
    return pltpu.make_async_copy(src_hbm.at[pl.ds(row, 1), :], dst_vmem.at[pl.ds(i, 1), :], sem)


GATHER_UNROLL = 8


def _moe_kernel(bexp_ref, tok_ref, nused_ref, h_hbm, wg_ref, wu_ref, wd_ref, y_ref,
                xbuf, wg_bf, wu_bf, wd_bf, sems):
    b = pl.program_id(0)
    n_used = nused_ref[0]
    slot = lax.rem(b, 2)

    def start_gather(blk, buf_slot):
        def body(i, carry):
            tok = tok_ref[blk * MOE_BLOCK + i]
            _row_gather_copy(h_hbm, xbuf.at[buf_slot], sems.at[buf_slot], tok, i).start()
            return carry
        lax.fori_loop(0, MOE_BLOCK, body, 0, unroll=GATHER_UNROLL)

    @pl.when(jnp.logical_and(b == 0, n_used > 0))
    def _():
        start_gather(0, 0)

    @pl.when(b + 1 < n_used)
    def _():
        start_gather(b + 1, 1 - slot)

    @pl.when(jnp.logical_or(b == 0, bexp_ref[b] != bexp_ref[jnp.maximum(b - 1, 0)]))
    def _():
        wg_bf[...] = wg_ref[0, 0].astype(BF16)
        wu_bf[...] = wu_ref[0, 0].astype(BF16)
        wd_bf[...] = wd_ref[0, 0].astype(BF16)

    @pl.when(b < n_used)
    def _():
        pltpu.make_async_copy(h_hbm.at[pl.ds(0, MOE_BLOCK), :], xbuf.at[slot], sems.at[slot]).wait()
        x = xbuf[slot].astype(BF16)
        g = jnp.dot(x, wg_bf[...], preferred_element_type=F32)
        u = jnp.dot(x, wu_bf[...], preferred_element_type=F32)
        hid = (_silu(g) * u).astype(BF16)
        y_ref[...] = jnp.dot(hid, wd_bf[...], preferred_element_type=F32)

    @pl.when(b >= n_used)
    def _():
        y_ref[...] = jnp.zeros_like(y_ref)


def _moe_experts(h, block_expert, slot_token, n_used, w_gate, w_up, w_down, layer):
    s, d = h.shape
    n_slots = slot_token.shape[0]
    n_blocks = n_slots // MOE_BLOCK
    dff = w_gate.shape[3]
    grid_spec = pltpu.PrefetchScalarGridSpec(
        num_scalar_prefetch=3,
        grid=(n_blocks,),
        in_specs=[pl.BlockSpec(memory_space=pl.ANY),
                  pl.BlockSpec((1, 1, d, dff), lambda b, be, tok, nu: (layer, be[b], 0, 0)),
                  pl.BlockSpec((1, 1, d, dff), lambda b, be, tok, nu: (layer, be[b], 0, 0)),
                  pl.BlockSpec((1, 1, dff, d), lambda b, be, tok, nu: (layer, be[b], 0, 0))],
        out_specs=pl.BlockSpec((MOE_BLOCK, d), lambda b, be, tok, nu: (b, 0)),
        scratch_shapes=[pltpu.VMEM((2, MOE_BLOCK, d), F32),
                        pltpu.VMEM((d, dff), BF16), pltpu.VMEM((d, dff), BF16), pltpu.VMEM((dff, d), BF16),
                        pltpu.SemaphoreType.DMA((2,))],
    )
    return pl.pallas_call(
        _moe_kernel,
        grid_spec=grid_spec,
        out_shape=jax.ShapeDtypeStruct((n_slots, d), F32),
        compiler_params=_cparams("arbitrary"),
        name="moe_experts",
    )(block_expert, slot_token, n_used, h, w_gate, w_up, w_down)


COMBINE_TILE = 256


def _combine_ln_kernel(pos_ref, y_hbm, h_ref, route_ref, gam_ref, bet_ref, hout_ref, hbf_ref, ybuf, sems):
    t = pl.program_id(0)
    n_t = pl.num_programs(0)
    slot = lax.rem(t, 2)

    def start_gather(tile, buf_slot):
        def body(i, carry):
            a = (tile * COMBINE_TILE + i) * MOE_TOP_K
            for k in range(MOE_TOP_K):
                _row_gather_copy(y_hbm, ybuf.at[buf_slot, k], sems.at[buf_slot], pos_ref[a + k], i).start()
            return carry
        lax.fori_loop(0, COMBINE_TILE, body, 0, unroll=GATHER_UNROLL)

    @pl.when(t == 0)
    def _():
        start_gather(0, 0)

    @pl.when(t + 1 < n_t)
    def _():
        start_gather(t + 1, 1 - slot)

    for k in range(MOE_TOP_K):
        pltpu.make_async_copy(y_hbm.at[pl.ds(0, COMBINE_TILE), :], ybuf.at[slot, k], sems.at[slot]).wait()
    route = route_ref[...]
    ffn = route[:, 2:3] * ybuf[slot, 0] + route[:, 3:4] * ybuf[slot, 1]
    hn = _layer_norm(DEEPNORM_ALPHA * h_ref[...] + ffn, gam_ref[...], bet_ref[...])
    hout_ref[...] = hn
    hbf_ref[...] = hn.astype(BF16)


def _combine_ln(y_slots, pos, h, route, gamma, beta):
    s, d = h.shape
    tm = COMBINE_TILE
    grid_spec = pltpu.PrefetchScalarGridSpec(
        num_scalar_prefetch=1,
        grid=(s // tm,),
        in_specs=[pl.BlockSpec(memory_space=pl.ANY),
                  pl.BlockSpec((tm, d), lambda i, pos: (i, 0)),
                  pl.BlockSpec((tm, LANES), lambda i, pos: (i, 0)),
                  pl.BlockSpec((1, d), lambda i, pos: (0, 0)),
                  pl.BlockSpec((1, d), lambda i, pos: (0, 0))],
        out_specs=[pl.BlockSpec((tm, d), lambda i, pos: (i, 0)),
                   pl.BlockSpec((tm, d), lambda i, pos: (i, 0))],
        scratch_shapes=[pltpu.VMEM((2, MOE_TOP_K, tm, d), F32),
                        pltpu.SemaphoreType.DMA((2,))],
    )
    return pl.pallas_call(
        _combine_ln_kernel,
        grid_spec=grid_spec,
        out_shape=[jax.ShapeDtypeStruct((s, d), F32), jax.ShapeDtypeStruct((s, d), BF16)],
        compiler_params=_cparams("arbitrary"),
        name="combine_ln",
    )(pos, y_slots, h, route, gamma, beta)


def _dispatch_tables(route):
    s = route.shape[0]
    n_asg = s * MOE_TOP_K
    expert = route[:, :MOE_TOP_K].astype(jnp.int32).reshape(n_asg)
    onehot = (expert[:, None] == jnp.arange(MOE_EXPERTS, dtype=jnp.int32)[None, :]).astype(jnp.int32)
    csum = jnp.cumsum(onehot, axis=0)
    counts = csum[-1]
    padded = (counts + MOE_BLOCK - 1) // MOE_BLOCK * MOE_BLOCK
    pend = jnp.cumsum(padded)
    pstart = pend - padded
    pos = jnp.sum(onehot * (csum - 1 + pstart[None, :]), axis=1).astype(jnp.int32)
    n_blocks = (n_asg + MOE_EXPERTS * (MOE_BLOCK - 1) + MOE_BLOCK - 1) // MOE_BLOCK
    n_slots = n_blocks * MOE_BLOCK
    token_ids = jnp.arange(n_asg, dtype=jnp.int32) // MOE_TOP_K
    slot_token = jnp.zeros((n_slots,), jnp.int32).at[pos].set(token_ids)
    block_start = jnp.arange(n_blocks, dtype=jnp.int32) * MOE_BLOCK
    block_expert = jnp.sum((pend[None, :] <= block_start[:, None]).astype(jnp.int32), axis=1)
    block_expert = jnp.minimum(block_expert, MOE_EXPERTS - 1).astype(jnp.int32)
    n_used = (pend[-1:] // MOE_BLOCK).astype(jnp.int32)
    return pos, slot_token, block_expert, n_used


def _moe_layer(h, route, w_gate, w_up, w_down, layer, gamma, beta):
    pos, slot_token, block_expert, n_used = _dispatch_tables(route)
    y_slots = _moe_experts(h, block_expert, slot_token, n_used, w_gate, w_up, w_down, layer)
    return _combine_ln(y_slots, pos, h, route, gamma, beta)


def _t5_bucket(dist):
    n = jnp.maximum(dist, 0)
    max_exact = REL_BUCKETS // 2
    ratio = jnp.maximum(n, max_exact).astype(F32) / max_exact
    large = max_exact + (jnp.log(ratio) / math.log(REL_MAX_DIST / max_exact) * (REL_BUCKETS - max_exact)).astype(jnp.int32)
    large = jnp.minimum(large, REL_BUCKETS - 1)
    return jnp.where(n < max_exact, n, large)


def _bias_kernel(idx_ref, tab_ref, o_ref):
    idx = idx_ref[0]
    tab = tab_ref[0]
    acc = jnp.zeros(o_ref.shape[1:], F32)
    for b in range(REL_BUCKETS):
        acc = jnp.where(idx == b, tab[:, b:b + 1], acc)
    p = lax.broadcasted_iota(jnp.int32, acc.shape, 1)
    ik = _mod_pow2(p, 2 * ATT_BLOCK)
    iq = lax.shift_right_logical(p, int(math.log2(2 * ATT_BLOCK)))
    rel = iq + ATT_BLOCK - ik
    o_ref[0] = jnp.where((rel >= 0) & (rel <= ATT_BLOCK), acc, -jnp.inf)


def _attention_bias(rel_bias):
    iq = jnp.arange(ATT_BLOCK)[:, None]
    ik = jnp.arange(2 * ATT_BLOCK)[None, :]
    rel = iq + ATT_BLOCK - ik
    idx = jnp.stack([_t5_bucket(rel * dil) for _, dil in DILATION_CONFIGS], 0)
    n_pairs = ATT_BLOCK * 2 * ATT_BLOCK
    idx = idx.reshape(N_DIL_GROUPS, 1, n_pairs).astype(jnp.int32)
    tab = jnp.transpose(rel_bias.astype(F32), (1, 2, 0))
    bias = pl.pallas_call(
        _bias_kernel,
        grid=(N_DIL_GROUPS,),
        in_specs=[pl.BlockSpec((1, 1, n_pairs), lambda g: (g, 0, 0)),
                  pl.BlockSpec((1, ATT_HEADS, REL_BUCKETS), lambda g: (g, 0, 0))],
        out_specs=pl.BlockSpec((1, ATT_HEADS, n_pairs), lambda g: (g, 0, 0)),
        out_shape=jax.ShapeDtypeStruct((N_DIL_GROUPS, ATT_HEADS, n_pairs), F32),
        compiler_params=_cparams("arbitrary"),
        name="attn_bias",
    )(idx, tab)
    return bias.reshape(N_DIL_GROUPS, ATT_HEADS, ATT_BLOCK, 2 * ATT_BLOCK)


def _attn_kernel(q_ref, kp_ref, kc_ref, vp_ref, vc_ref, bias_ref, o_ref, lse_ref):
    n = pl.program_id(1)
    pair_w = 2 * ATT_HEAD_DIM
    assert pair_w == LANES
    dn = (((1,), (1,)), ((), ()))

    def run(use_prev):
        n_keys = (2 if use_prev else 1) * ATT_BLOCK
        lane_k = lax.broadcasted_iota(jnp.int32, (n_keys, pair_w), 1)
        head_a = lane_k < ATT_HEAD_DIM
        ones_a = jnp.where(head_a, 1.0, 0.0).astype(BF16)
        ones_b = jnp.where(head_a, 0.0, 1.0).astype(BF16)
        lane_q = lax.broadcasted_iota(jnp.int32, (ATT_BLOCK, pair_w), 1)
        zero = jnp.zeros((n_keys, pair_w), BF16)
        for p in range(ATT_HEADS // 2):
            ps = slice(p * pair_w, (p + 1) * pair_w)
            q = q_ref[:, ps] * (ATT_HEAD_DIM ** -0.5)
            if use_prev:
                k = jnp.concatenate([kp_ref[:, ps], kc_ref[:, ps]], axis=0)
                v = jnp.concatenate([vp_ref[:, ps], vc_ref[:, ps]], axis=0)
            else:
                k = kc_ref[:, ps]
                v = vc_ref[:, ps]
            k_ab = jnp.concatenate([jnp.where(head_a, k, zero), jnp.where(head_a, zero, k)], axis=0)
            s = lax.dot_general(q, k_ab, dn, preferred_element_type=F32)
            key0 = 0 if use_prev else ATT_BLOCK
            s_a = s[:, :n_keys] + bias_ref[0, 2 * p][:, key0:]
            s_b = s[:, n_keys:] + bias_ref[0, 2 * p + 1][:, key0:]
            if use_prev:
                m_a = jnp.max(jnp.maximum(s_a[:, :ATT_BLOCK], s_a[:, ATT_BLOCK:]), axis=-1, keepdims=True)
                m_b = jnp.max(jnp.maximum(s_b[:, :ATT_BLOCK], s_b[:, ATT_BLOCK:]), axis=-1, keepdims=True)
            else:
                m_a = jnp.max(s_a, axis=-1, keepdims=True)
                m_b = jnp.max(s_b, axis=-1, keepdims=True)
            p_a = jnp.exp(s_a - m_a).astype(BF16)
            p_b = jnp.exp(s_b - m_b).astype(BF16)
            rhs_a = jnp.concatenate([jnp.where(head_a, v, zero), ones_a], axis=1)
            rhs_b = jnp.concatenate([jnp.where(head_a, zero, v), ones_b], axis=1)
            acc = (jnp.dot(p_a, rhs_a, preferred_element_type=F32)
                   + jnp.dot(p_b, rhs_b, preferred_element_type=F32))
            l = acc[:, pair_w:]
            o_ref[:, ps] = (acc[:, :pair_w] / l).astype(o_ref.dtype)
            lse_ref[:, ps] = jnp.where(lane_q < ATT_HEAD_DIM, m_a, m_b) + jnp.log(l)

    @pl.when(n > 0)
    def _():
        run(True)

    @pl.when(n == 0)
    def _():
        run(False)


def _dilated_branch(qkv, bias, g, dilation):
    s = qkv.shape[0]
    hd = ATT_HEADS * ATT_HEAD_DIM
    sub_len = s // dilation
    n_blk = sub_len // ATT_BLOCK
    assert s % dilation == 0 and sub_len % ATT_BLOCK == 0

    def cur(j):
        return pl.BlockSpec((ATT_BLOCK, hd), lambda r, n: (r * n_blk + n, j))

    def prev(j):
        return pl.BlockSpec((ATT_BLOCK, hd), lambda r, n: (r * n_blk + jnp.maximum(n - 1, 0), j))

    out_spec = pl.BlockSpec((ATT_BLOCK, hd), lambda r, n: (r * n_blk + n, 0))
    return pl.pallas_call(
        _attn_kernel,
        grid=(dilation, n_blk),
        in_specs=[cur(0), prev(1), cur(1), prev(2), cur(2),
                  pl.BlockSpec((1, ATT_HEADS, ATT_BLOCK, 2 * ATT_BLOCK), lambda r, n: (g, 0, 0, 0))],
        out_specs=[out_spec, out_spec],
        out_shape=[jax.ShapeDtypeStruct((s, hd), BF16), jax.ShapeDtypeStruct((s, hd), F32)],
        compiler_params=_cparams("arbitrary", "arbitrary"),
        name=f"dilated_attn_g{g}",
    )(qkv, qkv, qkv, qkv, qkv, bias)


def _attn_combine_kernel(o0, o1, o2, l0, l1, l2, out_ref):
    a0, a1, a2 = l0[...], l1[...], l2[...]
    m = jnp.maximum(jnp.maximum(a0, a1), a2)
    e0, e1, e2 = jnp.exp(a0 - m), jnp.exp(a1 - m), jnp.exp(a2 - m)
    num = e0 * o0[...].astype(F32) + e1 * o1[...].astype(F32) + e2 * o2[...].astype(F32)
    out_ref[...] = (num / (e0 + e1 + e2)).astype(out_ref.dtype)


def _attn_combine(outs, lses, tm=512):
    s, d = outs[0].shape
    spec = pl.BlockSpec((tm, d), lambda i: (i, 0))
    return pl.pallas_call(
        _attn_combine_kernel,
        grid=(s // tm,),
        in_specs=[spec] * 6,
        out_specs=spec,
        out_shape=jax.ShapeDtypeStruct((s, d), BF16),
        compiler_params=_cparams("arbitrary"),
        name="attn_combine",
    )(*outs, *lses)


def _to_strided_order(x, dilation):
    s, c = x.shape
    return x.reshape(s // dilation, dilation, c).transpose(1, 0, 2).reshape(s, c)


def _from_strided_order(x, dilation):
    s, c = x.shape
    return x.reshape(dilation, s // dilation, c).transpose(1, 0, 2).reshape(s, c)


def _dilated_attention(h_bf, w_in_bf, bias):
    per_group = 3 * ATT_HEADS * ATT_HEAD_DIM
    outs, lses = [], []
    for g, (_, dilation) in enumerate(DILATION_CONFIGS):
        qkv = _matmul(_to_strided_order(h_bf, dilation), w_in_bf[:, g * per_group:(g + 1) * per_group], BF16)
        o, l = _dilated_branch(qkv, bias, g, dilation)
        outs.append(_from_strided_order(o, dilation))
        lses.append(_from_strided_order(l, dilation))
    return _attn_combine(outs, lses)


def _head_expand(n_heads, width):
    r = lax.broadcasted_iota(jnp.int32, (LANES, n_heads * width), 0)
    c = lax.broadcasted_iota(jnp.int32, (LANES, n_heads * width), 1)
    return (_floor_to_pow2(c, width) == r * width).astype(F32)


def _causal_conv_silu(cur, halo, w, b):
    rows = lax.broadcasted_iota(jnp.int32, (SUBLANES, cur.shape[1]), 0)
    acc = cur * w[SSM_CONV - 1:SSM_CONV, :] + b
    for j in range(1, SSM_CONV):
        shifted = pltpu.roll(cur, j, 0)
        head = jnp.where(rows < j, pltpu.roll(halo, j, 0), shifted[:SUBLANES])
        shifted = jnp.concatenate([head, shifted[SUBLANES:]], axis=0)
        acc = acc + shifted * w[SSM_CONV - 1 - j:SSM_CONV - j, :]
    return _silu(acc)


def _ssd_kernel(z_ref, x_ref, bc_ref, xh_ref, bch_ref, dt_ref, cwx_ref, cbx_ref, cwbc_ref, cbbc_ref,
                dtb_ref, aneg_ref, dskip_ref, nw_ref, o_ref, state_ref):
    c = pl.program_id(0)

    @pl.when(c == 0)
    def _():
        state_ref[...] = jnp.zeros_like(state_ref)

    first = c == 0
    xh = jnp.where(first, 0.0, xh_ref[...].astype(F32))
    bch = jnp.where(first, 0.0, bch_ref[...].astype(F32))
    xs = _causal_conv_silu(x_ref[...].astype(F32), xh, cwx_ref[...], cbx_ref[...])
    bc = _causal_conv_silu(bc_ref[...].astype(F32), bch, cwbc_ref[...], cbbc_ref[...])
    gn = SSM_GROUPS * SSM_STATE

    x_dt = dt_ref[...] + dtb_ref[...]
    dt = jnp.maximum(x_dt, 0.0) + jnp.log1p(jnp.exp(-jnp.abs(x_dt)))
    a = dt * aneg_ref[...]
    row = lax.broadcasted_iota(jnp.int32, (SSM_CHUNK, SSM_CHUNK), 0)
    colv = lax.broadcasted_iota(jnp.int32, (SSM_CHUNK, SSM_CHUNK), 1)
    causal = colv <= row
    a_cs = jnp.dot(causal.astype(F32), a, precision=HIGHEST, preferred_element_type=F32)
    a_cs_t = a_cs.T
    expand = _head_expand(SSM_HEADS, SSM_HEAD_DIM)
    a_last = a_cs[SSM_CHUNK - 1:SSM_CHUNK, :]
    dt_e = jnp.dot(dt, expand, precision=HIGHEST, preferred_element_type=F32)
    in_decay_e = jnp.dot(jnp.exp(a_cs), expand, precision=HIGHEST, preferred_element_type=F32)
    out_decay_e = jnp.dot(jnp.exp(a_last - a_cs), expand, precision=HIGHEST, preferred_element_type=F32)
    chunk_decay_e = in_decay_e[SSM_CHUNK - 1:SSM_CHUNK, :]

    xdt = xs * dt_e
    xdt_bf = xdt.astype(BF16)
    xend_bf = (xdt * out_decay_e).astype(BF16)
    gw = SSM_HEADS_PER_GROUP * SSM_HEAD_DIM
    ys = []
    for g in range(SSM_GROUPS):
        b_g = bc[:, g * SSM_STATE:(g + 1) * SSM_STATE]
        c_g = bc[:, gn + g * SSM_STATE:gn + (g + 1) * SSM_STATE].astype(BF16)
        cb = lax.dot_general(c_g, b_g.astype(BF16), (((1,), (1,)), ((), ())), preferred_element_type=F32)
        prev = state_ref[g]
        y_off = jnp.dot(c_g, prev.astype(BF16), preferred_element_type=F32) * in_decay_e[:, g * gw:(g + 1) * gw]
        y_parts = []
        for r in range(SSM_HEADS_PER_GROUP):
            hh = g * SSM_HEADS_PER_GROUP + r
            seg = jnp.exp(jnp.where(causal, a_cs[:, hh:hh + 1] - a_cs_t[hh:hh + 1, :], -jnp.inf))
            y_parts.append(jnp.dot((cb * seg).astype(BF16), xdt_bf[:, hh * SSM_HEAD_DIM:(hh + 1) * SSM_HEAD_DIM],
                                   preferred_element_type=F32))
        ys.append(jnp.concatenate(y_parts, axis=1) + y_off)
        new = jnp.dot(b_g.T.astype(BF16), xend_bf[:, g * gw:(g + 1) * gw], preferred_element_type=F32)
        state_ref[g] = prev * chunk_decay_e[:, g * gw:(g + 1) * gw] + new
    y = jnp.concatenate(ys, axis=1) + xs * dskip_ref[...]
    u = y * _silu(z_ref[...].astype(F32))
    gsz = SSM_D_INNER // SSM_GROUPS
    outs = []
    for g in range(SSM_GROUPS):
        ug = u[:, g * gsz:(g + 1) * gsz]
        outs.append(ug * lax.rsqrt(jnp.mean(ug * ug, axis=-1, keepdims=True) + NORM_EPS))
    o_ref[...] = (jnp.concatenate(outs, axis=1) * nw_ref[...]).astype(o_ref.dtype)


def _mamba2_inner(zxbc, dt_raw, conv_w, conv_b, dt_bias, a_log, d_skip, norm_w):
    s = zxbc.shape[0]
    n_c = s // SSM_CHUNK
    di = SSM_D_INNER
    halo_blocks = SSM_CHUNK // SUBLANES

    def pad_lanes(v):
        return jnp.pad(v.astype(F32), (0, LANES - v.shape[0])).reshape(1, LANES)

    cw = conv_w.astype(F32)
    cb = conv_b.astype(F32).reshape(1, -1)
    dskip = jnp.repeat(d_skip.astype(F32), SSM_HEAD_DIM).reshape(1, di)
    a_neg = pad_lanes(-jnp.exp(a_log.astype(F32)))

    def halo(colblk):
        return pl.BlockSpec((SUBLANES, di), lambda c: (jnp.maximum(c * halo_blocks - 1, 0), colblk))

    full = lambda shape: pl.BlockSpec(shape, lambda c: (0, 0))
    return pl.pallas_call(
        _ssd_kernel,
        grid=(n_c,),
        in_specs=[pl.BlockSpec((SSM_CHUNK, di), lambda c: (c, 0)),
                  pl.BlockSpec((SSM_CHUNK, di), lambda c: (c, 1)),
                  pl.BlockSpec((SSM_CHUNK, SSM_BC_DIM), lambda c: (c, 2)),
                  halo(1), halo(2),
                  pl.BlockSpec((SSM_CHUNK, LANES), lambda c: (c, 0)),
                  full((SSM_CONV, di)), full((1, di)), full((SSM_CONV, SSM_BC_DIM)), full((1, SSM_BC_DIM)),
                  full((1, LANES)), full((1, LANES)), full((1, di)), full((1, di))],
        out_specs=pl.BlockSpec((SSM_CHUNK, di), lambda c: (c, 0)),
        out_shape=jax.ShapeDtypeStruct((s, di), BF16),
        scratch_shapes=[pltpu.VMEM((SSM_GROUPS, SSM_STATE, SSM_HEADS_PER_GROUP * SSM_HEAD_DIM), F32)],
        compiler_params=_cparams("arbitrary"),
        name="ssd_chunk_scan",
    )(zxbc, zxbc, zxbc, zxbc, zxbc, dt_raw, cw[:, :di], cb[:, :di], cw[:, di:], cb[:, di:],
      pad_lanes(dt_bias), a_neg, dskip, norm_w.astype(F32).reshape(1, di))


def _mamba2_mixer(h_bf, w_in, conv_w, conv_b, dt_bias, a_log, d_skip, norm_w):
    n_main = SSM_D_INNER + SSM_D_INNER + SSM_BC_DIM
    assert SSM_BC_DIM == SSM_D_INNER
    w_main = w_in[:, :n_main].astype(BF16)
    w_dt = jnp.pad(w_in[:, n_main:], ((0, 0), (0, LANES - SSM_HEADS))).astype(BF16)
    zxbc = _matmul(h_bf, w_main, BF16)
    dt_raw = _matmul(h_bf, w_dt, F32)
    return _mamba2_inner(zxbc, dt_raw, conv_w, conv_b, dt_bias, a_log, d_skip, norm_w)


def _gla_kernel(q_ref, f_ref, v_ref, gate_ref, lb_ref, nw_ref, o_ref, state_ref):
    c = pl.program_id(0)

    @pl.when(c == 0)
    def _():
        state_ref[...] = jnp.zeros_like(state_ref)

    L = GLA_CHUNK
    K = HGRN_HEAD_DIM
    n_sub = L // GLA_SUB
    row = lax.broadcasted_iota(jnp.int32, (L, L), 0)
    col = lax.broadcasted_iota(jnp.int32, (L, L), 1)
    tri = (col <= row).astype(F32)
    sub3 = lax.broadcasted_iota(jnp.int32, (n_sub, GLA_SUB, K), 1)
    lane3 = lax.broadcasted_iota(jnp.int32, (n_sub, GLA_SUB, L), 2)
    base3 = lax.broadcasted_iota(jnp.int32, (n_sub, GLA_SUB, L), 0) * GLA_SUB
    rowk = lax.broadcasted_iota(jnp.int32, (L, K), 0)
    levels = []
    m = 2 * GLA_SUB
    while m <= L:
        half = m // 2
        second = _mod_pow2(rowk, m) >= half
        sel = (col == _floor_to_pow2(row, m) + half - 1).astype(F32)
        same = _floor_to_pow2(row, m) == _floor_to_pow2(col, m)
        levels.append((second, sel, same))
        m *= 2

    for h in range(HGRN_HEADS):
        hs = slice(h * K, (h + 1) * K)
        f = f_ref[:, hs]
        lb = lb_ref[:, hs]
        q = _silu(q_ref[:, hs].astype(F32))
        v_bf = v_ref[:, hs]
        l1p = jnp.log1p(jnp.exp(-jnp.abs(f)))
        k = (1.0 - lb) * jnp.exp(jnp.minimum(-f, 0.0) - l1p)
        ta = jnp.log(lb)
        tb = jnp.log1p(-lb) + jnp.minimum(f, 0.0) - l1p
        log_f = jnp.maximum(ta, tb) + jnp.log1p(jnp.exp(-jnp.abs(ta - tb)))
        g = jnp.dot(tri, log_f, precision=HIGHEST, preferred_element_type=F32)

        g3 = g.reshape(n_sub, GLA_SUB, K)
        q3 = q.reshape(n_sub, GLA_SUB, K)
        k3 = k.reshape(n_sub, GLA_SUB, K)
        a3 = jnp.zeros((n_sub, GLA_SUB, L), F32)
        for s in range(GLA_SUB):
            e = jnp.exp(jnp.where(sub3 >= s, g3 - g3[:, s:s + 1, :], -jnp.inf))
            val = jnp.sum(q3 * e * k3[:, s:s + 1, :], axis=-1, keepdims=True)
            a3 = a3 + jnp.where(lane3 == base3 + s, val, 0.0)
        att = a3.reshape(L, L)

        for second, sel, same in levels:
            gb = jnp.dot(sel, g, precision=HIGHEST, preferred_element_type=F32)
            ql = (q * jnp.exp(jnp.where(second, g - gb, -jnp.inf))).astype(BF16)
            kl = (k * jnp.exp(jnp.where(second, -jnp.inf, gb - g))).astype(BF16)
            al = lax.dot_general(ql, kl, (((1,), (1,)), ((), ())), preferred_element_type=F32)
            att = att + jnp.where(same, al, 0.0)

        state_t = state_ref[h]
        o = (jnp.dot(att.astype(BF16), v_bf, preferred_element_type=F32)
             + lax.dot_general((q * jnp.exp(g)).astype(BF16), state_t.astype(BF16), (((1,), (1,)), ((), ())),
                               preferred_element_type=F32))
        g_last = g[L - 1:L, :]
        k_dec = (k * jnp.exp(g_last - g)).astype(BF16)
        v_t = v_bf.astype(F32).T.astype(BF16)
        state_ref[h] = state_t * jnp.exp(g_last) + jnp.dot(v_t, k_dec, preferred_element_type=F32)
        o = o * lax.rsqrt(jnp.mean(o * o, axis=-1, keepdims=True) + NORM_EPS) * nw_ref[...]
        o_ref[:, hs] = (o * _silu(gate_ref[:, hs].astype(F32))).astype(o_ref.dtype)


def _gla(qig, f, lb, norm_w):
    s, d = f.shape
    n_c = s // GLA_CHUNK
    return pl.pallas_call(
        _gla_kernel,
        grid=(n_c,),
        in_specs=[pl.BlockSpec((GLA_CHUNK, d), lambda c: (c, 0)),
                  pl.BlockSpec((GLA_CHUNK, d), lambda c: (c, 0)),
                  pl.BlockSpec((GLA_CHUNK, d), lambda c: (c, 1)),
                  pl.BlockSpec((GLA_CHUNK, d), lambda c: (c, 2)),
                  pl.BlockSpec((1, d), lambda c: (0, 0)),
                  pl.BlockSpec((1, HGRN_HEAD_DIM), lambda c: (0, 0))],
        out_specs=pl.BlockSpec((GLA_CHUNK, d), lambda c: (c, 0)),
        out_shape=jax.ShapeDtypeStruct((s, d), BF16),
        scratch_shapes=[pltpu.VMEM((HGRN_HEADS, HGRN_HEAD_DIM, HGRN_HEAD_DIM), F32)],
        compiler_params=_cparams("arbitrary"),
        name="gla_chunk_scan",
    )(qig, f, qig, qig, lb, norm_w)


def _hgrn2_mixer(h_bf, w_in, lb, norm_w):
    d = D_MODEL
    w_q, w_f, w_i, w_g = (w_in[:, j * d:(j + 1) * d] for j in range(4))
    qig = _matmul(h_bf, jnp.concatenate([w_q, w_i, w_g], axis=1).astype(BF16), BF16)
    f = _matmul(h_bf, w_f.astype(BF16), F32)
    return _gla(qig, f, lb.astype(F32).reshape(1, d), norm_w.astype(F32).reshape(1, HGRN_HEAD_DIM))


def kernel(x, attn_w_in, attn_w_out, rel_bias, ssm_w_in, ssm_conv_w, ssm_conv_b, ssm_dt_bias, ssm_a_log, ssm_d, ssm_norm_w, ssm_w_out, hgrn_w_in, hgrn_lower_bound, hgrn_norm_w, hgrn_w_out, moe_w_coarse, moe_w_fine, moe_w_gate, moe_w_up, moe_w_down, ln_gamma, ln_beta):
    b_, s_, d_ = x.shape
    assert b_ == 1 and d_ == D_MODEL
    lbs = jax.nn.softmax(hgrn_lower_bound.astype(F32), axis=0)
    lbs = jnp.cumsum(lbs, axis=0) - lbs[0]
    bias = _attention_bias(rel_bias)
    h = x.reshape(s_, d_).astype(F32)
    h_bf = h.astype(BF16)
    for layer in range(DEPTH):
        kind = layer % N_MIXERS
        j = layer // N_MIXERS
        if kind == 0:
            a = _dilated_attention(h_bf, attn_w_in[j].astype(BF16), bias)
            w_out = attn_w_out[j]
        elif kind == 1:
            a = _mamba2_mixer(h_bf, ssm_w_in[j], ssm_conv_w[j], ssm_conv_b[j], ssm_dt_bias[j], ssm_a_log[j],
                              ssm_d[j], ssm_norm_w[j])
            w_out = ssm_w_out[j]
        else:
            a = _hgrn2_mixer(h_bf, hgrn_w_in[j], lbs[layer], hgrn_norm_w[j])
            w_out = hgrn_w_out[j]
        gam = ln_gamma[layer].astype(F32)
        bet = ln_beta[layer].astype(F32)
        w_route = jnp.pad(jnp.concatenate([moe_w_coarse[layer], moe_w_fine[layer]], axis=1).astype(F32),
                          ((0, 0), (0, LANES - MOE_GROUPS - MOE_EXPERTS)))
        h, route = _mix_ln_route(a, w_out.astype(BF16), h, gam[0:1], bet[0:1], w_route)
        h, h_bf = _moe_layer(h, route, moe_w_gate.astype(F32), moe_w_up.astype(F32), moe_w_down.astype(F32), layer,
                             gam[1:2], bet[1:2])
    return h.reshape(b_, s_, d_)
```

```python
import functools
import math

import jax
import jax.numpy as jnp
from jax import lax
from jax.experimental import pallas as pl
from jax.experimental.pallas import tpu as pltpu

F32 = jnp.float32
BF16 = jnp.bfloat16
HIGHEST = lax.Precision.HIGHEST

D_MODEL = 1024
DEPTH = 4
N_MIXERS = 3
DEEPNORM_ALPHA = (2 * DEPTH) ** 0.25
NORM_EPS = 1e-5

ATT_HEAD_DIM = 64
ATT_HEADS = D_MODEL // ATT_HEAD_DIM
DILATION_CONFIGS = ((128, 1), (512, 4), (2048, 16))
N_DIL_GROUPS = len(DILATION_CONFIGS)
ATT_BLOCK = 128
REL_BUCKETS = 32
REL_MAX_DIST = 2048

SSM_D_INNER = 2 * D_MODEL
SSM_HEAD_DIM = 64
SSM_HEADS = SSM_D_INNER // SSM_HEAD_DIM
SSM_GROUPS = 8
SSM_HEADS_PER_GROUP = SSM_HEADS // SSM_GROUPS
SSM_STATE = 128
SSM_CONV = 4
SSM_CHUNK = 128
SSM_BC_DIM = 2 * SSM_GROUPS * SSM_STATE

HGRN_HEAD_DIM = 128
HGRN_HEADS = D_MODEL // HGRN_HEAD_DIM
GLA_CHUNK = 128
GLA_SUB = 8

MOE_GROUPS = 8
MOE_EXPERTS_PER_GROUP = 8
MOE_EXPERTS = MOE_GROUPS * MOE_EXPERTS_PER_GROUP
MOE_TOP_K = 2
MOE_D_FF = 512
MOE_BLOCK = 256

LANES = 128
SUBLANES = 8
VMEM_LIMIT_BYTES = 52 * 1024 * 1024


def _cparams(*sem):
    return pltpu.CompilerParams(dimension_semantics=sem, vmem_limit_bytes=VMEM_LIMIT_BYTES)


def _silu(x):
    return x / (1.0 + jnp.exp(-x))


def _split_bf16(v, parts):
    out = []
    for _ in range(parts - 1):
        p = v.astype(BF16)
        out.append(p)
        v = v - p.astype(F32)
    out.append(v.astype(BF16))
    return out


def _dot_01_left(sel3, v):
    return jnp.dot(sel3, jnp.concatenate(_split_bf16(v, 3), axis=0), preferred_element_type=F32)


def _dot_01_right(v, sel3):
    return jnp.dot(jnp.concatenate(_split_bf16(v, 3), axis=1), sel3, preferred_element_type=F32)


def _floor_to_pow2(v, m):
    assert m & (m - 1) == 0
    return jnp.bitwise_and(v, -m)


def _mod_pow2(v, m):
    assert m & (m - 1) == 0
    return jnp.bitwise_and(v, m - 1)


def _mm_kernel(x_ref, w_ref, o_ref):
    o_ref[...] = jnp.dot(x_ref[...], w_ref[...], preferred_element_type=F32).astype(o_ref.dtype)


def _matmul(x, w, out_dtype, tm=1024, tn=1024):
    m, k = x.shape
    n = w.shape[1]
    tm = min(tm, m)
    tn = min(tn, n)
    assert m % tm == 0 and n % tn == 0
    return pl.pallas_call(
        _mm_kernel,
        grid=(n // tn, m // tm),
        in_specs=[pl.BlockSpec((tm, k), lambda j, i: (i, 0)),
                  pl.BlockSpec((k, tn), lambda j, i: (0, j))],
        out_specs=pl.BlockSpec((tm, tn), lambda j, i: (i, j)),
        out_shape=jax.ShapeDtypeStruct((m, n), out_dtype),
        compiler_params=_cparams("arbitrary", "arbitrary"),
        name="mm",
    )(x, w)


def _layer_norm(y, gamma, beta):
    mu = jnp.mean(y, axis=-1, keepdims=True)
    yc = y - mu
    var = jnp.mean(yc * yc, axis=-1, keepdims=True)
    return yc * lax.rsqrt(var + NORM_EPS) * gamma + beta


def _mix_ln_route_kernel(a_ref, w_ref, h_ref, gam_ref, bet_ref, wr_ref, hout_ref, route_ref):
    mix = jnp.dot(a_ref[...], w_ref[...], preferred_element_type=F32)
    hn = _layer_norm(DEEPNORM_ALPHA * h_ref[...] + mix, gam_ref[...], bet_ref[...])
    hout_ref[...] = hn
    hn_hi, hn_lo = _split_bf16(hn, 2)
    both = jnp.dot(hn_hi, wr_ref[...], preferred_element_type=F32)
    logits = (both[:, :LANES] + both[:, LANES:]
              + jnp.dot(hn_lo, wr_ref[:, :LANES], preferred_element_type=F32))
    lane = lax.broadcasted_iota(jnp.int32, logits.shape, 1).astype(F32)
    neg = -jnp.inf
    l1 = jnp.where(lane < MOE_GROUPS, logits, neg)
    m1 = jnp.max(l1, axis=-1, keepdims=True)
    grp = jnp.min(jnp.where(l1 == m1, lane, float(LANES)), axis=-1, keepdims=True)
    g1 = 1.0 / jnp.sum(jnp.exp(l1 - m1), axis=-1, keepdims=True)
    lo = MOE_GROUPS + grp * MOE_EXPERTS_PER_GROUP
    l2 = jnp.where(lane >= lo, jnp.where(lane < lo + MOE_EXPERTS_PER_GROUP, logits, neg), neg)
    t1 = jnp.max(l2, axis=-1, keepdims=True)
    i1 = jnp.min(jnp.where(l2 == t1, lane, float(LANES)), axis=-1, keepdims=True)
    l2b = jnp.where(lane == i1, neg, l2)
    t2 = jnp.max(l2b, axis=-1, keepdims=True)
    i2 = jnp.min(jnp.where(l2b == t2, lane, float(LANES)), axis=-1, keepdims=True)
    e21 = jnp.exp(t2 - t1)
    ga = g1 / (1.0 + e21)
    gb = g1 * e21 / (1.0 + e21)
    out = jnp.where(lane == 0, i1 - MOE_GROUPS,
                    jnp.where(lane == 1, i2 - MOE_GROUPS,
                              jnp.where(lane == 2, ga, jnp.where(lane == 3, gb, 0.0))))
    route_ref[...] = out


def _router_weights(w_coarse, w_fine):
    w = jnp.pad(jnp.concatenate([w_coarse, w_fine], axis=1).astype(F32),
                ((0, 0), (0, LANES - MOE_GROUPS - MOE_EXPERTS)))
    hi = w.astype(BF16)
    lo = (w - hi.astype(F32)).astype(BF16)
    return jnp.concatenate([hi, lo], axis=1)


def _mix_ln_route(a, w, h, gamma, beta, w_route, tm=512):
    s, k = a.shape
    d = w.shape[1]
    return pl.pallas_call(
        _mix_ln_route_kernel,
        grid=(s // tm,),
        in_specs=[pl.BlockSpec((tm, k), lambda i: (i, 0)),
                  pl.BlockSpec((k, d), lambda i: (0, 0)),
                  pl.BlockSpec((tm, d), lambda i: (i, 0)),
                  pl.BlockSpec((1, d), lambda i: (0, 0)),
                  pl.BlockSpec((1, d), lambda i: (0, 0)),
                  pl.BlockSpec((d, 2 * LANES), lambda i: (0, 0))],
        out_specs=[pl.BlockSpec((tm, d), lambda i: (i, 0)),
                   pl.BlockSpec((tm, LANES), lambda i: (i, 0))],
        out_shape=[jax.ShapeDtypeStruct((s, d), F32),
                   jax.ShapeDtypeStruct((s, LANES), F32)],
        compiler_params=_cparams("arbitrary"),
        name="mix_ln_route",
    )(a, w, h, gamma, beta, w_route)


def _start_row_gather(src_hbm, row_of, dst_vmem, sem):
    for i in range(dst_vmem.shape[0]):
        pltpu.make_async_copy(src_hbm.at[pl.ds(row_of(i), 1), :], dst_vmem.at[pl.ds(i, 1), :], sem).start()


def _wait_row_gather(dst_vmem, sem):
    pltpu.make_async_copy(dst_vmem, dst_vmem, sem).wait()


def _moe_kernel(bexp_ref, tok_ref, nused_ref, h_hbm, wg_ref, wu_ref, wd_ref, y_ref,
                xbuf0, xbuf1, wg_bf, wu_bf, wd_bf, sems):
    b = pl.program_id(0)
    n_used = nused_ref[0]
    bufs = (xbuf0, xbuf1)

    def start_gather(blk, parity):
        _start_row_gather(h_hbm, lambda i: tok_ref[blk * MOE_BLOCK + i], bufs[parity], sems.at[parity])

    @pl.when(jnp.logical_and(b == 0, n_used > 0))
    def _():
        start_gather(0, 0)

    @pl.when(jnp.logical_or(b == 0, bexp_ref[b] != bexp_ref[jnp.maximum(b - 1, 0)]))
    def _():
        wg_bf[...] = wg_ref[0, 0].astype(BF16)
        wu_bf[...] = wu_ref[0, 0].astype(BF16)
        wd_bf[...] = wd_ref[0, 0].astype(BF16)

    for parity in range(2):
        @pl.when(jnp.logical_and(b < n_used, lax.rem(b, 2) == parity))
        def _():
            _wait_row_gather(bufs[parity], sems.at[parity])
            start_gather(jnp.minimum(b + 1, n_used - 1), 1 - parity)
            x = bufs[parity][...].astype(BF16)
            g = jnp.dot(x, wg_bf[...], preferred_element_type=F32)
            u = jnp.dot(x, wu_bf[...], preferred_element_type=F32)
            hid = (_silu(g) * u).astype(BF16)
            y_ref[...] = jnp.dot(hid, wd_bf[...], preferred_element_type=F32)

        @pl.when(jnp.logical_and(b == n_used - 1, lax.rem(b, 2) == parity))
        def _():
            _wait_row_gather(bufs[1 - parity], sems.at[1 - parity])

    @pl.when(b >= n_used)
    def _():
        y_ref[...] = jnp.zeros_like(y_ref)


def _moe_experts(h, block_expert, slot_token, n_used, w_gate, w_up, w_down, layer):
    s, d = h.shape
    n_slots = slot_token.shape[0]
    n_blocks = n_slots // MOE_BLOCK
    dff = w_gate.shape[3]
    grid_spec = pltpu.PrefetchScalarGridSpec(
        num_scalar_prefetch=3,
        grid=(n_blocks,),
        in_specs=[pl.BlockSpec(memory_space=pl.ANY),
                  pl.BlockSpec((1, 1, d, dff), lambda b, be, tok, nu: (layer, be[b], 0, 0)),
                  pl.BlockSpec((1, 1, d, dff), lambda b, be, tok, nu: (layer, be[b], 0, 0)),
                  pl.BlockSpec((1, 1, dff, d), lambda b, be, tok, nu: (layer, be[b], 0, 0))],
        out_specs=pl.BlockSpec((MOE_BLOCK, d), lambda b, be, tok, nu: (b, 0)),
        scratch_shapes=[pltpu.VMEM((MOE_BLOCK, d), F32), pltpu.VMEM((MOE_BLOCK, d), F32),
                        pltpu.VMEM((d, dff), BF16), pltpu.VMEM((d, dff), BF16), pltpu.VMEM((dff, d), BF16),
                        pltpu.SemaphoreType.DMA((2,))],
    )
    return pl.pallas_call(
        _moe_kernel,
        grid_spec=grid_spec,
        out_shape=jax.ShapeDtypeStruct((n_slots, d), F32),
        compiler_params=_cparams("arbitrary"),
        name="moe_experts",
    )(block_expert, slot_token, n_used, h, w_gate, w_up, w_down)


COMBINE_TILE = 256


def _combine_ln_kernel(pos_ref, y_hbm, h_ref, route_ref, gam_ref, bet_ref, hout_ref, hbf_ref,
                       ybuf00, ybuf01, ybuf10, ybuf11, sems):
    t = pl.program_id(0)
    n_t = pl.num_programs(0)
    bufs = ((ybuf00, ybuf01), (ybuf10, ybuf11))

    def start_gather(tile, parity):
        for k in range(MOE_TOP_K):
            _start_row_gather(y_hbm, lambda i: pos_ref[(tile * COMBINE_TILE + i) * MOE_TOP_K + k],
                              bufs[parity][k], sems.at[parity])

    @pl.when(t == 0)
    def _():
        start_gather(0, 0)

    for parity in range(2):
        @pl.when(lax.rem(t, 2) == parity)
        def _():
            for k in range(MOE_TOP_K):
                _wait_row_gather(bufs[parity][k], sems.at[parity])
            start_gather(jnp.minimum(t + 1, n_t - 1), 1 - parity)
            route = route_ref[...]
            ffn = route[:, 2:3] * bufs[parity][0][...] + route[:, 3:4] * bufs[parity][1][...]
            hn = _layer_norm(DEEPNORM_ALPHA * h_ref[...] + ffn, gam_ref[...], bet_ref[...])
            hout_ref[...] = hn
            hbf_ref[...] = hn.astype(BF16)

        @pl.when(jnp.logical_and(t == n_t - 1, lax.rem(t, 2) == parity))
        def _():
            for k in range(MOE_TOP_K):
                _wait_row_gather(bufs[1 - parity][k], sems.at[1 - parity])


def _combine_ln(y_slots, pos, h, route, gamma, beta):
    s, d = h.shape
    tm = COMBINE_TILE
    grid_spec = pltpu.PrefetchScalarGridSpec(
        num_scalar_prefetch=1,
        grid=(s // tm,),
        in_specs=[pl.BlockSpec(memory_space=pl.ANY),
                  pl.BlockSpec((tm, d), lambda i, pos: (i, 0)),
                  pl.BlockSpec((tm, LANES), lambda i, pos: (i, 0)),
                  pl.BlockSpec((1, d), lambda i, pos: (0, 0)),
                  pl.BlockSpec((1, d), lambda i, pos: (0, 0))],
        out_specs=[pl.BlockSpec((tm, d), lambda i, pos: (i, 0)),
                   pl.BlockSpec((tm, d), lambda i, pos: (i, 0))],
        scratch_shapes=[pltpu.VMEM((tm, d), F32)] * (2 * MOE_TOP_K) + [pltpu.SemaphoreType.DMA((2,))],
    )
    return pl.pallas_call(
        _combine_ln_kernel,
        grid_spec=grid_spec,
        out_shape=[jax.ShapeDtypeStruct((s, d), F32), jax.ShapeDtypeStruct((s, d), BF16)],
        compiler_params=_cparams("arbitrary"),
        name="combine_ln",
    )(pos, y_slots, h, route, gamma, beta)


def _dispatch_tables(route):
    s = route.shape[0]
    n_asg = s * MOE_TOP_K
    expert = route[:, :MOE_TOP_K].astype(jnp.int32).reshape(n_asg)
    onehot = (expert[:, None] == jnp.arange(MOE_EXPERTS, dtype=jnp.int32)[None, :]).astype(jnp.int32)
    csum = jnp.cumsum(onehot, axis=0)
    counts = csum[-1]
    padded = (counts + MOE_BLOCK - 1) // MOE_BLOCK * MOE_BLOCK
    pend = jnp.cumsum(padded)
    pstart = pend - padded
    pos = jnp.sum(onehot * (csum - 1 + pstart[None, :]), axis=1).astype(jnp.int32)
    n_blocks = (n_asg + MOE_EXPERTS * (MOE_BLOCK - 1) + MOE_BLOCK - 1) // MOE_BLOCK
    n_slots = n_blocks * MOE_BLOCK
    token_ids = jnp.arange(n_asg, dtype=jnp.int32) // MOE_TOP_K
    slot_token = jnp.zeros((n_slots,), jnp.int32).at[pos].set(token_ids)
    block_start = jnp.arange(n_blocks, dtype=jnp.int32) * MOE_BLOCK
    block_expert = jnp.sum((pend[None, :] <= block_start[:, None]).astype(jnp.int32), axis=1)
    block_expert = jnp.minimum(block_expert, MOE_EXPERTS - 1).astype(jnp.int32)
    n_used = (pend[-1:] // MOE_BLOCK).astype(jnp.int32)
    return pos, slot_token, block_expert, n_used


def _moe_layer(h, route, w_gate, w_up, w_down, layer, gamma, beta):
    pos, slot_token, block_expert, n_used = _dispatch_tables(route)
    y_slots = _moe_experts(h, block_expert, slot_token, n_used, w_gate, w_up, w_down, layer)
    return _combine_ln(y_slots, pos, h, route, gamma, beta)


def _t5_bucket(dist):
    n = jnp.maximum(dist, 0)
    max_exact = REL_BUCKETS // 2
    ratio = jnp.maximum(n, max_exact).astype(F32) / max_exact
    large = max_exact + (jnp.log(ratio) / math.log(REL_MAX_DIST / max_exact) * (REL_BUCKETS - max_exact)).astype(jnp.int32)
    large = jnp.minimum(large, REL_BUCKETS - 1)
    return jnp.where(n < max_exact, n, large)


def _bias_kernel(idx_ref, tab_ref, o_ref):
    idx = idx_ref[0]
    tab = tab_ref[0]
    acc = jnp.zeros(o_ref.shape[1:], F32)
    for b in range(REL_BUCKETS):
        acc = jnp.where(idx == b, tab[:, b:b + 1], acc)
    p = lax.broadcasted_iota(jnp.int32, acc.shape, 1)
    ik = _mod_pow2(p, 2 * ATT_BLOCK)
    iq = lax.shift_right_logical(p, int(math.log2(2 * ATT_BLOCK)))
    rel = iq + ATT_BLOCK - ik
    o_ref[0] = jnp.where((rel >= 0) & (rel <= ATT_BLOCK), acc, -jnp.inf)


def _attention_bias(rel_bias):
    iq = jnp.arange(ATT_BLOCK)[:, None]
    ik = jnp.arange(2 * ATT_BLOCK)[None, :]
    rel = iq + ATT_BLOCK - ik
    idx = jnp.stack([_t5_bucket(rel * dil) for _, dil in DILATION_CONFIGS], 0)
    n_pairs = ATT_BLOCK * 2 * ATT_BLOCK
    idx = idx.reshape(N_DIL_GROUPS, 1, n_pairs).astype(jnp.int32)
    tab = jnp.transpose(rel_bias.astype(F32), (1, 2, 0))
    bias = pl.pallas_call(
        _bias_kernel,
        grid=(N_DIL_GROUPS,),
        in_specs=[pl.BlockSpec((1, 1, n_pairs), lambda g: (g, 0, 0)),
                  pl.BlockSpec((1, ATT_HEADS, REL_BUCKETS), lambda g: (g, 0, 0))],
        out_specs=pl.BlockSpec((1, ATT_HEADS, n_pairs), lambda g: (g, 0, 0)),
        out_shape=jax.ShapeDtypeStruct((N_DIL_GROUPS, ATT_HEADS, n_pairs), F32),
        compiler_params=_cparams("arbitrary"),
        name="attn_bias",
    )(idx, tab)
    return bias.reshape(N_DIL_GROUPS, ATT_HEADS, ATT_BLOCK, 2 * ATT_BLOCK)


def _attn_kernel(q_ref, kp_ref, kc_ref, vp_ref, vc_ref, bias_ref, o_ref, lse_ref):
    n = pl.program_id(1)
    pair_w = 2 * ATT_HEAD_DIM
    assert pair_w == LANES
    dn = (((1,), (1,)), ((), ()))

    def run(use_prev):
        n_keys = (2 if use_prev else 1) * ATT_BLOCK
        lane_k = lax.broadcasted_iota(jnp.int32, (n_keys, pair_w), 1)
        head_a = lane_k < ATT_HEAD_DIM
        ones_a = jnp.where(head_a, 1.0, 0.0).astype(BF16)
        ones_b = jnp.where(head_a, 0.0, 1.0).astype(BF16)
        lane_q = lax.broadcasted_iota(jnp.int32, (ATT_BLOCK, pair_w), 1)
        zero = jnp.zeros((n_keys, pair_w), BF16)
        for p in range(ATT_HEADS // 2):
            ps = slice(p * pair_w, (p + 1) * pair_w)
            q = q_ref[:, ps] * (ATT_HEAD_DIM ** -0.5)
            if use_prev:
                k = jnp.concatenate([kp_ref[:, ps], kc_ref[:, ps]], axis=0)
                v = jnp.concatenate([vp_ref[:, ps], vc_ref[:, ps]], axis=0)
            else:
                k = kc_ref[:, ps]
                v = vc_ref[:, ps]
            k_ab = jnp.concatenate([jnp.where(head_a, k, zero), jnp.where(head_a, zero, k)], axis=0)
            s = lax.dot_general(q, k_ab, dn, preferred_element_type=F32)
            key0 = 0 if use_prev else ATT_BLOCK
            s_a = s[:, :n_keys] + bias_ref[0, 2 * p][:, key0:]
            s_b = s[:, n_keys:] + bias_ref[0, 2 * p + 1][:, key0:]
            if use_prev:
                m_a = jnp.max(jnp.maximum(s_a[:, :ATT_BLOCK], s_a[:, ATT_BLOCK:]), axis=-1, keepdims=True)
                m_b = jnp.max(jnp.maximum(s_b[:, :ATT_BLOCK], s_b[:, ATT_BLOCK:]), axis=-1, keepdims=True)
            else:
                m_a = jnp.max(s_a, axis=-1, keepdims=True)
                m_b = jnp.max(s_b, axis=-1, keepdims=True)
            p_a = jnp.exp(s_a - m_a).astype(BF16)
            p_b = jnp.exp(s_b - m_b).astype(BF16)
            rhs_a = jnp.concatenate([jnp.where(head_a, v, zero), ones_a], axis=1)
            rhs_b = jnp.concatenate([jnp.where(head_a, zero, v), ones_b], axis=1)
            acc = (jnp.dot(p_a, rhs_a, preferred_element_type=F32)
                   + jnp.dot(p_b, rhs_b, preferred_element_type=F32))
            l = acc[:, pair_w:]
            o_ref[:, ps] = (acc[:, :pair_w] / l).astype(o_ref.dtype)
            lse_ref[:, ps] = jnp.where(lane_q < ATT_HEAD_DIM, m_a, m_b) + jnp.log(l)

    @pl.when(n > 0)
    def _():
        run(True)

    @pl.when(n == 0)
    def _():
        run(False)


def _dilated_branch(qkv, bias, g, dilation):
    s = qkv.shape[0]
    hd = ATT_HEADS * ATT_HEAD_DIM
    sub_len = s // dilation
    n_blk = sub_len // ATT_BLOCK
    assert s % dilation == 0 and sub_len % ATT_BLOCK == 0

    def cur(j):
        return pl.BlockSpec((ATT_BLOCK, hd), lambda r, n: (r * n_blk + n, j))

    def prev(j):
        return pl.BlockSpec((ATT_BLOCK, hd), lambda r, n: (r * n_blk + jnp.maximum(n - 1, 0), j))

    out_spec = pl.BlockSpec((ATT_BLOCK, hd), lambda r, n: (r * n_blk + n, 0))
    return pl.pallas_call(
        _attn_kernel,
        grid=(dilation, n_blk),
        in_specs=[cur(0), prev(1), cur(1), prev(2), cur(2),
                  pl.BlockSpec((1, ATT_HEADS, ATT_BLOCK, 2 * ATT_BLOCK), lambda r, n: (g, 0, 0, 0))],
        out_specs=[out_spec, out_spec],
        out_shape=[jax.ShapeDtypeStruct((s, hd), BF16), jax.ShapeDtypeStruct((s, hd), F32)],
        compiler_params=_cparams("arbitrary", "arbitrary"),
        name=f"dilated_attn_g{g}",
    )(qkv, qkv, qkv, qkv, qkv, bias)


def _attn_combine_kernel(o0, o1, o2, l0, l1, l2, out_ref):
    a0, a1, a2 = l0[...], l1[...], l2[...]
    m = jnp.maximum(jnp.maximum(a0, a1), a2)
    e0, e1, e2 = jnp.exp(a0 - m), jnp.exp(a1 - m), jnp.exp(a2 - m)
    num = e0 * o0[...].astype(F32) + e1 * o1[...].astype(F32) + e2 * o2[...].astype(F32)
    out_ref[...] = (num / (e0 + e1 + e2)).astype(out_ref.dtype)


def _attn_combine(outs, lses, tm=512):
    s, d = outs[0].shape
    spec = pl.BlockSpec((tm, d), lambda i: (i, 0))
    return pl.pallas_call(
        _attn_combine_kernel,
        grid=(s // tm,),
        in_specs=[spec] * 6,
        out_specs=spec,
        out_shape=jax.ShapeDtypeStruct((s, d), BF16),
        compiler_params=_cparams("arbitrary"),
        name="attn_combine",
    )(*outs, *lses)


def _to_strided_order(x, dilation):
    s, c = x.shape
    return x.reshape(s // dilation, dilation, c).transpose(1, 0, 2).reshape(s, c)


def _from_strided_order(x, dilation):
    s, c = x.shape
    return x.reshape(dilation, s // dilation, c).transpose(1, 0, 2).reshape(s, c)


def _dilated_attention(h_bf, w_in_bf, bias):
    per_group = 3 * ATT_HEADS * ATT_HEAD_DIM
    outs, lses = [], []
    for g, (_, dilation) in enumerate(DILATION_CONFIGS):
        qkv = _matmul(_to_strided_order(h_bf, dilation), w_in_bf[:, g * per_group:(g + 1) * per_group], BF16)
        o, l = _dilated_branch(qkv, bias, g, dilation)
        outs.append(_from_strided_order(o, dilation))
        lses.append(_from_strided_order(l, dilation))
    return _attn_combine(outs, lses)


def _head_expand(n_heads, width):
    r = lax.broadcasted_iota(jnp.int32, (LANES, n_heads * width), 0)
    c = lax.broadcasted_iota(jnp.int32, (LANES, n_heads * width), 1)
    return jnp.where(_floor_to_pow2(c, width) == r * width, 1.0, 0.0).astype(BF16)


def _causal_conv_silu(cur, halo, w, b):
    rows = lax.broadcasted_iota(jnp.int32, (SUBLANES, cur.shape[1]), 0)
    acc = cur * w[SSM_CONV - 1:SSM_CONV, :] + b
    for j in range(1, SSM_CONV):
        shifted = pltpu.roll(cur, j, 0)
        head = jnp.where(rows < j, pltpu.roll(halo, j, 0), shifted[:SUBLANES])
        shifted = jnp.concatenate([head, shifted[SUBLANES:]], axis=0)
        acc = acc + shifted * w[SSM_CONV - 1 - j:SSM_CONV - j, :]
    return _silu(acc)


def _ssd_kernel(z_ref, x_ref, bc_ref, xh_ref, bch_ref, dt_ref, cwx_ref, cbx_ref, cwbc_ref, cbbc_ref,
                dtb_ref, aneg_ref, dskip_ref, nw_ref, o_ref, state_ref):
    c = pl.program_id(0)

    @pl.when(c == 0)
    def _():
        state_ref[...] = jnp.zeros_like(state_ref)

    first = c == 0
    xh = jnp.where(first, 0.0, xh_ref[...].astype(F32))
    bch = jnp.where(first, 0.0, bch_ref[...].astype(F32))
    xs = _causal_conv_silu(x_ref[...].astype(F32), xh, cwx_ref[...], cbx_ref[...])
    bc = _causal_conv_silu(bc_ref[...].astype(F32), bch, cwbc_ref[...], cbbc_ref[...])
    gn = SSM_GROUPS * SSM_STATE

    x_dt = dt_ref[...] + dtb_ref[...]
    dt = jnp.maximum(x_dt, 0.0) + jnp.log1p(jnp.exp(-jnp.abs(x_dt)))
    a = dt * aneg_ref[...]
    row = lax.broadcasted_iota(jnp.int32, (SSM_CHUNK, SSM_CHUNK), 0)
    colv = lax.broadcasted_iota(jnp.int32, (SSM_CHUNK, SSM_CHUNK), 1)
    causal = colv <= row
    tri = jnp.where(causal, 1.0, 0.0).astype(BF16)
    a_cs = _dot_01_left(jnp.concatenate([tri] * 3, axis=1), a)
    a_cs_t = a_cs.T
    expand = _head_expand(SSM_HEADS, SSM_HEAD_DIM)
    a_last = a_cs[SSM_CHUNK - 1:SSM_CHUNK, :]
    per_head = jnp.concatenate([dt, jnp.exp(a_cs), jnp.exp(a_last - a_cs)], axis=0)
    per_head_e = _dot_01_right(per_head, jnp.concatenate([expand] * 3, axis=0))
    dt_e = per_head_e[:SSM_CHUNK]
    in_decay_e = per_head_e[SSM_CHUNK:2 * SSM_CHUNK]
    out_decay_e = per_head_e[2 * SSM_CHUNK:]
    chunk_decay_e = in_decay_e[SSM_CHUNK - 1:SSM_CHUNK, :]

    xdt = xs * dt_e
    xdt_bf = xdt.astype(BF16)
    xend_bf = (xdt * out_decay_e).astype(BF16)
    gw = SSM_HEADS_PER_GROUP * SSM_HEAD_DIM
    ys = []
    for g in range(SSM_GROUPS):
        b_g = bc[:, g * SSM_STATE:(g + 1) * SSM_STATE]
        c_g = bc[:, gn + g * SSM_STATE:gn + (g + 1) * SSM_STATE].astype(BF16)
        cb = lax.dot_general(c_g, b_g.astype(BF16), (((1,), (1,)), ((), ())), preferred_element_type=F32)
        prev = state_ref[g]
        y_off = jnp.dot(c_g, prev.astype(BF16), preferred_element_type=F32) * in_decay_e[:, g * gw:(g + 1) * gw]
        y_parts = []
        for r in range(SSM_HEADS_PER_GROUP):
            hh = g * SSM_HEADS_PER_GROUP + r
            seg = jnp.exp(jnp.where(causal, a_cs[:, hh:hh + 1] - a_cs_t[hh:hh + 1, :], -jnp.inf))
            y_parts.append(jnp.dot((cb * seg).astype(BF16), xdt_bf[:, hh * SSM_HEAD_DIM:(hh + 1) * SSM_HEAD_DIM],
                                   preferred_element_type=F32))
        ys.append(jnp.concatenate(y_parts, axis=1) + y_off)
        new = jnp.dot(b_g.T.astype(BF16), xend_bf[:, g * gw:(g + 1) * gw], preferred_element_type=F32)
        state_ref[g] = prev * chunk_decay_e[:, g * gw:(g + 1) * gw] + new
    y = jnp.concatenate(ys, axis=1) + xs * dskip_ref[...]
    u = y * _silu(z_ref[...].astype(F32))
    gsz = SSM_D_INNER // SSM_GROUPS
    outs = []
    for g in range(SSM_GROUPS):
        ug = u[:, g * gsz:(g + 1) * gsz]
        outs.append(ug * lax.rsqrt(jnp.mean(ug * ug, axis=-1, keepdims=True) + NORM_EPS))
    o_ref[...] = (jnp.concatenate(outs, axis=1) * nw_ref[...]).astype(o_ref.dtype)


def _mamba2_inner(zxbc, dt_raw, conv_w, conv_b, dt_bias, a_log, d_skip, norm_w):
    s = zxbc.shape[0]
    n_c = s // SSM_CHUNK
    di = SSM_D_INNER
    halo_blocks = SSM_CHUNK // SUBLANES

    def pad_lanes(v):
        return jnp.pad(v.astype(F32), (0, LANES - v.shape[0])).reshape(1, LANES)

    cw = conv_w.astype(F32)
    cb = conv_b.astype(F32).reshape(1, -1)
    dskip = jnp.repeat(d_skip.astype(F32), SSM_HEAD_DIM).reshape(1, di)
    a_neg = pad_lanes(-jnp.exp(a_log.astype(F32)))

    def halo(colblk):
        return pl.BlockSpec((SUBLANES, di), lambda c: (jnp.maximum(c * halo_blocks - 1, 0), colblk))

    full = lambda shape: pl.BlockSpec(shape, lambda c: (0, 0))
    return pl.pallas_call(
        _ssd_kernel,
        grid=(n_c,),
        in_specs=[pl.BlockSpec((SSM_CHUNK, di), lambda c: (c, 0)),
                  pl.BlockSpec((SSM_CHUNK, di), lambda c: (c, 1)),
                  pl.BlockSpec((SSM_CHUNK, SSM_BC_DIM), lambda c: (c, 2)),
                  halo(1), halo(2),
                  pl.BlockSpec((SSM_CHUNK, LANES), lambda c: (c, 0)),
                  full((SSM_CONV, di)), full((1, di)), full((SSM_CONV, SSM_BC_DIM)), full((1, SSM_BC_DIM)),
                  full((1, LANES)), full((1, LANES)), full((1, di)), full((1, di))],
        out_specs=pl.BlockSpec((SSM_CHUNK, di), lambda c: (c, 0)),
        out_shape=jax.ShapeDtypeStruct((s, di), BF16),
        scratch_shapes=[pltpu.VMEM((SSM_GROUPS, SSM_STATE, SSM_HEADS_PER_GROUP * SSM_HEAD_DIM), F32)],
        compiler_params=_cparams("arbitrary"),
        name="ssd_chunk_scan",
    )(zxbc, zxbc, zxbc, zxbc, zxbc, dt_raw, cw[:, :di], cb[:, :di], cw[:, di:], cb[:, di:],
      pad_lanes(dt_bias), a_neg, dskip, norm_w.astype(F32).reshape(1, di))


def _mamba2_mixer(h_bf, w_in, conv_w, conv_b, dt_bias, a_log, d_skip, norm_w):
    n_main = SSM_D_INNER + SSM_D_INNER + SSM_BC_DIM
    assert SSM_BC_DIM == SSM_D_INNER
    w_main = w_in[:, :n_main].astype(BF16)
    w_dt = jnp.pad(w_in[:, n_main:], ((0, 0), (0, LANES - SSM_HEADS))).astype(BF16)
    zxbc = _matmul(h_bf, w_main, BF16)
    dt_raw = _matmul(h_bf, w_dt, F32)
    return _mamba2_inner(zxbc, dt_raw, conv_w, conv_b, dt_bias, a_log, d_skip, norm_w)


def _gla_kernel(q_ref, f_ref, v_ref, gate_ref, lb_ref, nw_ref, o_ref, state_ref):
    c = pl.program_id(0)

    @pl.when(c == 0)
    def _():
        state_ref[...] = jnp.zeros_like(state_ref)

    L = GLA_CHUNK
    K = HGRN_HEAD_DIM
    n_sub = L // GLA_SUB
    row = lax.broadcasted_iota(jnp.int32, (L, L), 0)
    col = lax.broadcasted_iota(jnp.int32, (L, L), 1)
    tri = jnp.where(col <= row, 1.0, 0.0).astype(BF16)
    tri3 = jnp.concatenate([tri] * 3, axis=1)
    sub3 = lax.broadcasted_iota(jnp.int32, (n_sub, GLA_SUB, L), 1)
    rel3 = (lax.broadcasted_iota(jnp.int32, (n_sub, GLA_SUB, L), 2)
            - lax.broadcasted_iota(jnp.int32, (n_sub, GLA_SUB, L), 0) * GLA_SUB)
    rel3 = jnp.where(rel3 >= 0, jnp.where(rel3 <= sub3, rel3, -1), -1)
    levels = []
    m = 2 * GLA_SUB
    while m <= L:
        levels.append(m)
        m *= 2
    same_block = {m: _floor_to_pow2(row, m) == _floor_to_pow2(col, m) for m in levels if m < L}

    for h in range(HGRN_HEADS):
        hs = slice(h * K, (h + 1) * K)
        f = f_ref[:, hs]
        lb = lb_ref[:, hs]
        q = _silu(q_ref[:, hs].astype(F32))
        v_bf = v_ref[:, hs]
        l1p = jnp.log1p(jnp.exp(-jnp.abs(f)))
        log_k = jnp.log1p(-lb) + jnp.minimum(-f, 0.0) - l1p
        ta = jnp.log(lb)
        tb = jnp.log1p(-lb) + jnp.minimum(f, 0.0) - l1p
        log_f = jnp.maximum(ta, tb) + jnp.log1p(jnp.exp(-jnp.abs(ta - tb)))
        g = _dot_01_left(tri3, log_f)
        gk = g - log_k

        g3 = g.reshape(n_sub, GLA_SUB, K)
        q3 = q.reshape(n_sub, GLA_SUB, K)
        gk3 = gk.reshape(n_sub, GLA_SUB, K)
        a3 = jnp.zeros((n_sub, GLA_SUB, L), F32)
        for s in range(GLA_SUB):
            e = jnp.exp(jnp.minimum(g3 - gk3[:, s:s + 1, :], 0.0))
            val = jnp.sum(q3 * e, axis=-1, keepdims=True)
            a3 = jnp.where(rel3 == s, val, a3)
        att = a3.reshape(L, L)

        for m in levels:
            half = m // 2
            zeros = jnp.zeros((half, K), F32)
            q_parts, k_parts = [], []
            for j in range(L // m):
                lo, mid, hi = j * m, j * m + half, (j + 1) * m
                gb = g[mid - 1:mid, :]
                q_parts += [zeros, q[mid:hi] * jnp.exp(g[mid:hi] - gb)]
                k_parts += [jnp.exp(gb - gk[lo:mid]), zeros]
            ql = jnp.concatenate(q_parts, axis=0).astype(BF16)
            kl = jnp.concatenate(k_parts, axis=0).astype(BF16)
            al = lax.dot_general(ql, kl, (((1,), (1,)), ((), ())), preferred_element_type=F32)
            att = att + (jnp.where(same_block[m], al, 0.0) if m < L else al)

        state_t = state_ref[h]
        o = (jnp.dot(att.astype(BF16), v_bf, preferred_element_type=F32)
             + lax.dot_general((q * jnp.exp(g)).astype(BF16), state_t.astype(BF16), (((1,), (1,)), ((), ())),
                               preferred_element_type=F32))
        g_last = g[L - 1:L, :]
        k_dec = jnp.exp(g_last - gk).astype(BF16)
        v_t = v_bf.astype(F32).T.astype(BF16)
        state_ref[h] = state_t * jnp.exp(g_last) + jnp.dot(v_t, k_dec, preferred_element_type=F32)
        o = o * lax.rsqrt(jnp.mean(o * o, axis=-1, keepdims=True) + NORM_EPS) * nw_ref[...]
        o_ref[:, hs] = (o * _silu(gate_ref[:, hs].astype(F32))).astype(o_ref.dtype)


def _gla(qig, f, lb, norm_w):
    s, d = f.shape
    n_c = s // GLA_CHUNK
    return pl.pallas_call(
        _gla_kernel,
        grid=(n_c,),
        in_specs=[pl.BlockSpec((GLA_CHUNK, d), lambda c: (c, 0)),
                  pl.BlockSpec((GLA_CHUNK, d), lambda c: (c, 0)),
                  pl.BlockSpec((GLA_CHUNK, d), lambda c: (c, 1)),
                  pl.BlockSpec((GLA_CHUNK, d), lambda c: (c, 2)),
                  pl.BlockSpec((1, d), lambda c: (0, 0)),
                  pl.BlockSpec((1, HGRN_HEAD_DIM), lambda c: (0, 0))],
        out_specs=pl.BlockSpec((GLA_CHUNK, d), lambda c: (c, 0)),
        out_shape=jax.ShapeDtypeStruct((s, d), BF16),
        scratch_shapes=[pltpu.VMEM((HGRN_HEADS, HGRN_HEAD_DIM, HGRN_HEAD_DIM), F32)],
        compiler_params=_cparams("arbitrary"),
        name="gla_chunk_scan",
    )(qig, f, qig, qig, lb, norm_w)


def _hgrn2_mixer(h_bf, w_in, lb, norm_w):
    d = D_MODEL
    w_q, w_f, w_i, w_g = (w_in[:, j * d:(j + 1) * d] for j in range(4))
    qig = _matmul(h_bf, jnp.concatenate([w_q, w_i, w_g], axis=1).astype(BF16), BF16)
    f = _matmul(h_bf, w_f.astype(BF16), F32)
    return _gla(qig, f, lb.astype(F32).reshape(1, d), norm_w.astype(F32).reshape(1, HGRN_HEAD_DIM))


def kernel(x, attn_w_in, attn_w_out, rel_bias, ssm_w_in, ssm_conv_w, ssm_conv_b, ssm_dt_bias, ssm_a_log, ssm_d, ssm_norm_w, ssm_w_out, hgrn_w_in, hgrn_lower_bound, hgrn_norm_w, hgrn_w_out, moe_w_coarse, moe_w_fine, moe_w_gate, moe_w_up, moe_w_down, ln_gamma, ln_beta):
    b_, s_, d_ = x.shape
    assert b_ == 1 and d_ == D_MODEL
    lbs = jax.nn.softmax(hgrn_lower_bound.astype(F32), axis=0)
    lbs = jnp.cumsum(lbs, axis=0) - lbs[0]
    bias = _attention_bias(rel_bias)
    h = x.reshape(s_, d_).astype(F32)
    h_bf = h.astype(BF16)
    for layer in range(DEPTH):
        kind = layer % N_MIXERS
        j = layer // N_MIXERS
        if kind == 0:
            a = _dilated_attention(h_bf, attn_w_in[j].astype(BF16), bias)
            w_out = attn_w_out[j]
        elif kind == 1:
            a = _mamba2_mixer(h_bf, ssm_w_in[j], ssm_conv_w[j], ssm_conv_b[j], ssm_dt_bias[j], ssm_a_log[j],
                              ssm_d[j], ssm_norm_w[j])
            w_out = ssm_w_out[j]
        else:
            a = _hgrn2_mixer(h_bf, hgrn_w_in[j], lbs[layer], hgrn_norm_w[j])
            w_out = hgrn_w_out[j]
        gam = ln_gamma[layer].astype(F32)
        bet = ln_beta[layer].astype(F32)
        w_route = _router_weights(moe_w_coarse[layer], moe_w_fine[layer])
        h, route = _mix_ln_route(a, w_out.astype(BF16), h, gam[0:1], bet[0:1], w_route)
        h, h_bf = _moe_layer(h, route, moe_w_gate.astype(F32), moe_w_up.astype(F32), moe_w_down.astype(F32), layer,
                             gam[1:2], bet[1:2])
    return h.reshape(b_, s_, d_)
```

```python
import functools
import math

import jax
import jax.numpy as jnp
from jax import lax
from jax.experimental import pallas as pl
from jax.experimental.pallas import tpu as pltpu

F32 = jnp.float32
BF16 = jnp.bfloat16
HIGHEST = lax.Precision.HIGHEST

D_MODEL = 1024
DEPTH = 4
N_MIXERS = 3
DEEPNORM_ALPHA = (2 * DEPTH) ** 0.25
NORM_EPS = 1e-5

ATT_HEAD_DIM = 64
ATT_HEADS = D_MODEL // ATT_HEAD_DIM
DILATION_CONFIGS = ((128, 1), (512, 4), (2048, 16))
N_DIL_GROUPS = len(DILATION_CONFIGS)
ATT_BLOCK = 128
REL_BUCKETS = 32
REL_MAX_DIST = 2048

SSM_D_INNER = 2 * D_MODEL
SSM_HEAD_DIM = 64
SSM_HEADS = SSM_D_INNER // SSM_HEAD_DIM
SSM_GROUPS = 8
SSM_HEADS_PER_GROUP = SSM_HEADS // SSM_GROUPS
SSM_STATE = 128
SSM_CONV = 4
SSM_CHUNK = 128
SSM_BC_DIM = 2 * SSM_GROUPS * SSM_STATE

HGRN_HEAD_DIM = 128
HGRN_HEADS = D_MODEL // HGRN_HEAD_DIM
GLA_CHUNK = 128
GLA_SUB = 8

MOE_GROUPS = 8
MOE_EXPERTS_PER_GROUP = 8
MOE_EXPERTS = MOE_GROUPS * MOE_EXPERTS_PER_GROUP
MOE_TOP_K = 2
MOE_D_FF = 512
MOE_BLOCK = 256

LANES = 128
SUBLANES = 8
VMEM_LIMIT_BYTES = 52 * 1024 * 1024


def _cparams(*sem):
    return pltpu.CompilerParams(dimension_semantics=sem, vmem_limit_bytes=VMEM_LIMIT_BYTES)


def _silu(x):
    return x / (1.0 + jnp.exp(-x))


def _split_bf16(v, parts):
    out = []
    for _ in range(parts - 1):
        p = v.astype(BF16)
        out.append(p)
        v = v - p.astype(F32)
    out.append(v.astype(BF16))
    return out


def _dot_01_left(sel3, v):
    return jnp.dot(sel3, jnp.concatenate(_split_bf16(v, 3), axis=0), preferred_element_type=F32)


def _dot_01_right(v, sel3):
    return jnp.dot(jnp.concatenate(_split_bf16(v, 3), axis=1), sel3, preferred_element_type=F32)


def _floor_to_pow2(v, m):
    assert m & (m - 1) == 0
    return jnp.bitwise_and(v, -m)


def _mod_pow2(v, m):
    assert m & (m - 1) == 0
    return jnp.bitwise_and(v, m - 1)


def _mm_kernel(x_ref, w_ref, o_ref):
    o_ref[...] = jnp.dot(x_ref[...], w_ref[...], preferred_element_type=F32).astype(o_ref.dtype)


def _matmul(x, w, out_dtype, tm=1024, tn=1024):
    m, k = x.shape
    n = w.shape[1]
    tm = min(tm, m)
    tn = min(tn, n)
    assert m % tm == 0 and n % tn == 0
    return pl.pallas_call(
        _mm_kernel,
        grid=(n // tn, m // tm),
        in_specs=[pl.BlockSpec((tm, k), lambda j, i: (i, 0)),
                  pl.BlockSpec((k, tn), lambda j, i: (0, j))],
        out_specs=pl.BlockSpec((tm, tn), lambda j, i: (i, j)),
        out_shape=jax.ShapeDtypeStruct((m, n), out_dtype),
        compiler_params=_cparams("arbitrary", "arbitrary"),
        name="mm",
    )(x, w)


def _layer_norm(y, gamma, beta):
    mu = jnp.mean(y, axis=-1, keepdims=True)
    yc = y - mu
    var = jnp.mean(yc * yc, axis=-1, keepdims=True)
    return yc * lax.rsqrt(var + NORM_EPS) * gamma + beta


def _mix_ln_route_kernel(a_ref, w_ref, h_ref, gam_ref, bet_ref, wr_ref, hout_ref, route_ref):
    mix = jnp.dot(a_ref[...], w_ref[...], preferred_element_type=F32)
    hn = _layer_norm(DEEPNORM_ALPHA * h_ref[...] + mix, gam_ref[...], bet_ref[...])
    hout_ref[...] = hn
    hn_hi, hn_lo = _split_bf16(hn, 2)
    both = jnp.dot(hn_hi, wr_ref[...], preferred_element_type=F32)
    logits = (both[:, :LANES] + both[:, LANES:]
              + jnp.dot(hn_lo, wr_ref[:, :LANES], preferred_element_type=F32))
    lane = lax.broadcasted_iota(jnp.int32, logits.shape, 1).astype(F32)
    neg = -jnp.inf
    l1 = jnp.where(lane < MOE_GROUPS, logits, neg)
    m1 = jnp.max(l1, axis=-1, keepdims=True)
    grp = jnp.min(jnp.where(l1 == m1, lane, float(LANES)), axis=-1, keepdims=True)
    g1 = 1.0 / jnp.sum(jnp.exp(l1 - m1), axis=-1, keepdims=True)
    lo = MOE_GROUPS + grp * MOE_EXPERTS_PER_GROUP
    l2 = jnp.where(lane >= lo, jnp.where(lane < lo + MOE_EXPERTS_PER_GROUP, logits, neg), neg)
    t1 = jnp.max(l2, axis=-1, keepdims=True)
    i1 = jnp.min(jnp.where(l2 == t1, lane, float(LANES)), axis=-1, keepdims=True)
    l2b = jnp.where(lane == i1, neg, l2)
    t2 = jnp.max(l2b, axis=-1, keepdims=True)
    i2 = jnp.min(jnp.where(l2b == t2, lane, float(LANES)), axis=-1, keepdims=True)
    e21 = jnp.exp(t2 - t1)
    ga = g1 / (1.0 + e21)
    gb = g1 * e21 / (1.0 + e21)
    out = jnp.where(lane == 0, i1 - MOE_GROUPS,
                    jnp.where(lane == 1, i2 - MOE_GROUPS,
                              jnp.where(lane == 2, ga, jnp.where(lane == 3, gb, 0.0))))
    route_ref[...] = out


def _router_weights(w_coarse, w_fine):
    w = jnp.pad(jnp.concatenate([w_coarse, w_fine], axis=1).astype(F32),
                ((0, 0), (0, LANES - MOE_GROUPS - MOE_EXPERTS)))
    hi = w.astype(BF16)
    lo = (w - hi.astype(F32)).astype(BF16)
    return jnp.concatenate([hi, lo], axis=1)


def _mix_ln_route(a, w, h, gamma, beta, w_route, tm=512):
    s, k = a.shape
    d = w.shape[1]
    return pl.pallas_call(
        _mix_ln_route_kernel,
        grid=(s // tm,),
        in_specs=[pl.BlockSpec((tm, k), lambda i: (i, 0)),
                  pl.BlockSpec((k, d), lambda i: (0, 0)),
                  pl.BlockSpec((tm, d), lambda i: (i, 0)),
                  pl.BlockSpec((1, d), lambda i: (0, 0)),
                  pl.BlockSpec((1, d), lambda i: (0, 0)),
                  pl.BlockSpec((d, 2 * LANES), lambda i: (0, 0))],
        out_specs=[pl.BlockSpec((tm, d), lambda i: (i, 0)),
                   pl.BlockSpec((tm, LANES), lambda i: (i, 0))],
        out_shape=[jax.ShapeDtypeStruct((s, d), F32),
                   jax.ShapeDtypeStruct((s, LANES), F32)],
        compiler_params=_cparams("arbitrary"),
        name="mix_ln_route",
    )(a, w, h, gamma, beta, w_route)


def _start_row_gather(src_hbm, row_of, dst_vmem, sem):
    for i in range(dst_vmem.shape[0]):
        pltpu.make_async_copy(src_hbm.at[pl.ds(row_of(i), 1), :], dst_vmem.at[pl.ds(i, 1), :], sem).start()


def _wait_row_gather(dst_vmem, sem):
    pltpu.make_async_copy(dst_vmem, dst_vmem, sem).wait()


MOE_GATHER_BUFFERS = 4
MOE_WEIGHT_SLOTS = 3


def _moe_kernel(bpos_ref, eseq_ref, tok_ref, cnt_ref, h_hbm, wg_hbm, wu_hbm, wd_hbm, y_ref,
                xbuf0, xbuf1, xbuf2, xbuf3, wg_f, wu_f, wd_f, wg_bf, wu_bf, wd_bf, gsems, wsems, *, layer):
    b = pl.program_id(0)
    n_used = cnt_ref[0]
    n_exp = cnt_ref[1]
    bufs = (xbuf0, xbuf1, xbuf2, xbuf3)
    depth = MOE_GATHER_BUFFERS - 1

    def start_gather(blk, r):
        _start_row_gather(h_hbm, lambda i: tok_ref[blk * MOE_BLOCK + i], bufs[r], gsems.at[r])

    def weight_copies(p):
        e = eseq_ref[p]
        slot = lax.rem(p, MOE_WEIGHT_SLOTS)
        return [pltpu.make_async_copy(w.at[layer, e], f.at[slot], wsems.at[slot])
                for w, f in ((wg_hbm, wg_f), (wu_hbm, wu_f), (wd_hbm, wd_f))]

    @pl.when(b == 0)
    def _():
        for r in range(depth):
            start_gather(jnp.minimum(r, n_used - 1), r)
        for c in weight_copies(0):
            c.start()

    @pl.when(jnp.logical_and(b == 0, n_exp > 1))
    def _():
        for c in weight_copies(1):
            c.start()

    p = bpos_ref[b]
    changed = jnp.logical_or(b == 0, p != bpos_ref[jnp.maximum(b - 1, 0)])

    @pl.when(jnp.logical_and(changed, p + 2 < n_exp))
    def _():
        for c in weight_copies(p + 2):
            c.start()

    @pl.when(changed)
    def _():
        for c in weight_copies(p):
            c.wait()
        slot = lax.rem(p, MOE_WEIGHT_SLOTS)
        wg_bf[...] = wg_f[slot].astype(BF16)
        wu_bf[...] = wu_f[slot].astype(BF16)
        wd_bf[...] = wd_f[slot].astype(BF16)

    for r in range(MOE_GATHER_BUFFERS):
        @pl.when(jnp.logical_and(b < n_used, lax.rem(b, MOE_GATHER_BUFFERS) == r))
        def _():
            _wait_row_gather(bufs[r], gsems.at[r])
            start_gather(jnp.minimum(b + depth, n_used - 1), (r + depth) % MOE_GATHER_BUFFERS)
            x = bufs[r][...].astype(BF16)
            g = jnp.dot(x, wg_bf[...], preferred_element_type=F32)
            u = jnp.dot(x, wu_bf[...], preferred_element_type=F32)
            hid = (_silu(g) * u).astype(BF16)
            y_ref[...] = jnp.dot(hid, wd_bf[...], preferred_element_type=F32)

        @pl.when(jnp.logical_and(b == n_used - 1, lax.rem(b, MOE_GATHER_BUFFERS) == r))
        def _():
            for ahead in range(1, MOE_GATHER_BUFFERS):
                rr = (r + ahead) % MOE_GATHER_BUFFERS
                _wait_row_gather(bufs[rr], gsems.at[rr])

    @pl.when(b >= n_used)
    def _():
        y_ref[...] = jnp.zeros_like(y_ref)


def _moe_experts(h, block_pos, expert_seq, slot_token, counts, w_gate, w_up, w_down, layer):
    s, d = h.shape
    n_slots = slot_token.shape[0]
    n_blocks = n_slots // MOE_BLOCK
    dff = w_gate.shape[3]
    hbm = pl.BlockSpec(memory_space=pl.ANY)
    grid_spec = pltpu.PrefetchScalarGridSpec(
        num_scalar_prefetch=4,
        grid=(n_blocks,),
        in_specs=[hbm, hbm, hbm, hbm],
        out_specs=pl.BlockSpec((MOE_BLOCK, d), lambda b, *_: (b, 0)),
        scratch_shapes=([pltpu.VMEM((MOE_BLOCK, d), F32)] * MOE_GATHER_BUFFERS
                        + [pltpu.VMEM((MOE_WEIGHT_SLOTS, d, dff), F32), pltpu.VMEM((MOE_WEIGHT_SLOTS, d, dff), F32),
                           pltpu.VMEM((MOE_WEIGHT_SLOTS, dff, d), F32),
                           pltpu.VMEM((d, dff), BF16), pltpu.VMEM((d, dff), BF16), pltpu.VMEM((dff, d), BF16),
                           pltpu.SemaphoreType.DMA((MOE_GATHER_BUFFERS,)),
                           pltpu.SemaphoreType.DMA((MOE_WEIGHT_SLOTS,))]),
    )
    return pl.pallas_call(
        functools.partial(_moe_kernel, layer=layer),
        grid_spec=grid_spec,
        out_shape=jax.ShapeDtypeStruct((n_slots, d), F32),
        compiler_params=_cparams("arbitrary"),
        name="moe_experts",
    )(block_pos, expert_seq, slot_token, counts, h, w_gate, w_up, w_down)


COMBINE_TILE = 256


COMBINE_BUFFERS = 3


def _combine_ln_kernel(pos_ref, y_hbm, h_ref, route_ref, gam_ref, bet_ref, hout_ref, hbf_ref, *scratch):
    t = pl.program_id(0)
    n_t = pl.num_programs(0)
    sems = scratch[-1]
    bufs = [scratch[r * MOE_TOP_K:(r + 1) * MOE_TOP_K] for r in range(COMBINE_BUFFERS)]
    depth = COMBINE_BUFFERS - 1

    def start_gather(tile, r):
        for k in range(MOE_TOP_K):
            _start_row_gather(y_hbm, lambda i: pos_ref[(tile * COMBINE_TILE + i) * MOE_TOP_K + k],
                              bufs[r][k], sems.at[r])

    def wait_gather(r):
        for k in range(MOE_TOP_K):
            _wait_row_gather(bufs[r][k], sems.at[r])

    @pl.when(t == 0)
    def _():
        for r in range(depth):
            start_gather(jnp.minimum(r, n_t - 1), r)

    for r in range(COMBINE_BUFFERS):
        @pl.when(lax.rem(t, COMBINE_BUFFERS) == r)
        def _():
            wait_gather(r)
            start_gather(jnp.minimum(t + depth, n_t - 1), (r + depth) % COMBINE_BUFFERS)
            route = route_ref[...]
            ffn = route[:, 2:3] * bufs[r][0][...] + route[:, 3:4] * bufs[r][1][...]
            hn = _layer_norm(DEEPNORM_ALPHA * h_ref[...] + ffn, gam_ref[...], bet_ref[...])
            hout_ref[...] = hn
            hbf_ref[...] = hn.astype(BF16)

        @pl.when(jnp.logical_and(t == n_t - 1, lax.rem(t, COMBINE_BUFFERS) == r))
        def _():
            for ahead in range(1, COMBINE_BUFFERS):
                wait_gather((r + ahead) % COMBINE_BUFFERS)


def _combine_ln(y_slots, pos, h, route, gamma, beta):
    s, d = h.shape
    tm = COMBINE_TILE
    grid_spec = pltpu.PrefetchScalarGridSpec(
        num_scalar_prefetch=1,
        grid=(s // tm,),
        in_specs=[pl.BlockSpec(memory_space=pl.ANY),
                  pl.BlockSpec((tm, d), lambda i, pos: (i, 0)),
                  pl.BlockSpec((tm, LANES), lambda i, pos: (i, 0)),
                  pl.BlockSpec((1, d), lambda i, pos: (0, 0)),
                  pl.BlockSpec((1, d), lambda i, pos: (0, 0))],
        out_specs=[pl.BlockSpec((tm, d), lambda i, pos: (i, 0)),
                   pl.BlockSpec((tm, d), lambda i, pos: (i, 0))],
        scratch_shapes=([pltpu.VMEM((tm, d), F32)] * (COMBINE_BUFFERS * MOE_TOP_K)
                        + [pltpu.SemaphoreType.DMA((COMBINE_BUFFERS,))]),
    )
    return pl.pallas_call(
        _combine_ln_kernel,
        grid_spec=grid_spec,
        out_shape=[jax.ShapeDtypeStruct((s, d), F32), jax.ShapeDtypeStruct((s, d), BF16)],
        compiler_params=_cparams("arbitrary"),
        name="combine_ln",
    )(pos, y_slots, h, route, gamma, beta)


def _dispatch_tables(route):
    s = route.shape[0]
    n_asg = s * MOE_TOP_K
    expert = route[:, :MOE_TOP_K].astype(jnp.int32).reshape(n_asg)
    onehot = (expert[:, None] == jnp.arange(MOE_EXPERTS, dtype=jnp.int32)[None, :]).astype(jnp.int32)
    csum = jnp.cumsum(onehot, axis=0)
    counts = csum[-1]
    padded = (counts + MOE_BLOCK - 1) // MOE_BLOCK * MOE_BLOCK
    pend = jnp.cumsum(padded)
    pstart = pend - padded
    pos = jnp.sum(onehot * (csum - 1 + pstart[None, :]), axis=1).astype(jnp.int32)
    n_blocks = (n_asg + MOE_EXPERTS * (MOE_BLOCK - 1) + MOE_BLOCK - 1) // MOE_BLOCK
    n_slots = n_blocks * MOE_BLOCK
    token_ids = jnp.arange(n_asg, dtype=jnp.int32) // MOE_TOP_K
    slot_token = jnp.zeros((n_slots,), jnp.int32).at[pos].set(token_ids)
    block_start = jnp.arange(n_blocks, dtype=jnp.int32) * MOE_BLOCK
    block_expert = jnp.sum((pend[None, :] <= block_start[:, None]).astype(jnp.int32), axis=1)
    block_expert = jnp.minimum(block_expert, MOE_EXPERTS - 1).astype(jnp.int32)
    present = counts > 0
    seq_pos = jnp.cumsum(present.astype(jnp.int32)) - 1
    n_exp = seq_pos[-1] + 1
    expert_ids = jnp.arange(MOE_EXPERTS, dtype=jnp.int32)
    expert_seq = jnp.zeros((MOE_EXPERTS,), jnp.int32).at[jnp.where(present, seq_pos, MOE_EXPERTS)].set(
        expert_ids, mode='drop')
    block_pos = jnp.sum(jnp.where(block_expert[:, None] == expert_ids[None, :], seq_pos[None, :], 0), axis=1)
    block_pos = jnp.minimum(block_pos, n_exp - 1).astype(jnp.int32)
    counts2 = jnp.stack([pend[-1] // MOE_BLOCK, n_exp]).astype(jnp.int32)
    return pos, slot_token, block_pos, expert_seq, counts2


def _moe_layer(h, route, w_gate, w_up, w_down, layer, gamma, beta):
    pos, slot_token, block_pos, expert_seq, counts = _dispatch_tables(route)
    y_slots = _moe_experts(h, block_pos, expert_seq, slot_token, counts, w_gate, w_up, w_down, layer)
    return _combine_ln(y_slots, pos, h, route, gamma, beta)


def _t5_bucket(dist):
    n = jnp.maximum(dist, 0)
    max_exact = REL_BUCKETS // 2
    ratio = jnp.maximum(n, max_exact).astype(F32) / max_exact
    large = max_exact + (jnp.log(ratio) / math.log(REL_MAX_DIST / max_exact) * (REL_BUCKETS - max_exact)).astype(jnp.int32)
    large = jnp.minimum(large, REL_BUCKETS - 1)
    return jnp.where(n < max_exact, n, large)


def _bias_kernel(idx_ref, tab_ref, o_ref):
    idx = idx_ref[0]
    tab = tab_ref[0]
    acc = jnp.zeros(o_ref.shape[1:], F32)
    for b in range(REL_BUCKETS):
        acc = jnp.where(idx == b, tab[:, b:b + 1], acc)
    p = lax.broadcasted_iota(jnp.int32, acc.shape, 1)
    ik = _mod_pow2(p, 2 * ATT_BLOCK)
    iq = lax.shift_right_logical(p, int(math.log2(2 * ATT_BLOCK)))
    rel = iq + ATT_BLOCK - ik
    o_ref[0] = jnp.where((rel >= 0) & (rel <= ATT_BLOCK), acc, -jnp.inf)


def _attention_bias(rel_bias):
    iq = jnp.arange(ATT_BLOCK)[:, None]
    ik = jnp.arange(2 * ATT_BLOCK)[None, :]
    rel = iq + ATT_BLOCK - ik
    idx = jnp.stack([_t5_bucket(rel * dil) for _, dil in DILATION_CONFIGS], 0)
    n_pairs = ATT_BLOCK * 2 * ATT_BLOCK
    idx = idx.reshape(N_DIL_GROUPS, 1, n_pairs).astype(jnp.int32)
    tab = jnp.transpose(rel_bias.astype(F32), (1, 2, 0))
    bias = pl.pallas_call(
        _bias_kernel,
        grid=(N_DIL_GROUPS,),
        in_specs=[pl.BlockSpec((1, 1, n_pairs), lambda g: (g, 0, 0)),
                  pl.BlockSpec((1, ATT_HEADS, REL_BUCKETS), lambda g: (g, 0, 0))],
        out_specs=pl.BlockSpec((1, ATT_HEADS, n_pairs), lambda g: (g, 0, 0)),
        out_shape=jax.ShapeDtypeStruct((N_DIL_GROUPS, ATT_HEADS, n_pairs), F32),
        compiler_params=_cparams("arbitrary"),
        name="attn_bias",
    )(idx, tab)
    return bias.reshape(N_DIL_GROUPS, ATT_HEADS, ATT_BLOCK, 2 * ATT_BLOCK)


LSE_LANES_PER_HEAD = LANES // ATT_HEADS


def _lse_lane(h):
    return (h & 1) * ATT_HEAD_DIM + (h >> 1) * LSE_LANES_PER_HEAD


def _attn_kernel(q_ref, kc_ref, vc_ref, bias_ref, o_ref, lse_ref, kprev, vprev):
    n = pl.program_id(1)
    pair_w = 2 * ATT_HEAD_DIM
    assert pair_w == LANES and 2 * LSE_LANES_PER_HEAD * (ATT_HEADS // 2) == LANES
    dn = (((1,), (1,)), ((), ()))

    def run(use_prev):
        n_keys = (2 if use_prev else 1) * ATT_BLOCK
        lane_k = lax.broadcasted_iota(jnp.int32, (n_keys, pair_w), 1)
        head_a = lane_k < ATT_HEAD_DIM
        ones_a = jnp.where(head_a, 1.0, 0.0).astype(BF16)
        ones_b = jnp.where(head_a, 0.0, 1.0).astype(BF16)
        lane_q = lax.broadcasted_iota(jnp.int32, (ATT_BLOCK, pair_w), 1)
        zero = jnp.zeros((n_keys, pair_w), BF16)
        lse_parts = []
        for p in range(ATT_HEADS // 2):
            ps = slice(p * pair_w, (p + 1) * pair_w)
            q = q_ref[:, ps] * (ATT_HEAD_DIM ** -0.5)
            if use_prev:
                k = jnp.concatenate([kprev[:, ps], kc_ref[:, ps]], axis=0)
                v = jnp.concatenate([vprev[:, ps], vc_ref[:, ps]], axis=0)
            else:
                k = kc_ref[:, ps]
                v = vc_ref[:, ps]
            k_ab = jnp.concatenate([jnp.where(head_a, k, zero), jnp.where(head_a, zero, k)], axis=0)
            s = lax.dot_general(q, k_ab, dn, preferred_element_type=F32)
            key0 = 0 if use_prev else ATT_BLOCK
            s_a = s[:, :n_keys] + bias_ref[0, 2 * p][:, key0:]
            s_b = s[:, n_keys:] + bias_ref[0, 2 * p + 1][:, key0:]
            if use_prev:
                m_a = jnp.max(jnp.maximum(s_a[:, :ATT_BLOCK], s_a[:, ATT_BLOCK:]), axis=-1, keepdims=True)
                m_b = jnp.max(jnp.maximum(s_b[:, :ATT_BLOCK], s_b[:, ATT_BLOCK:]), axis=-1, keepdims=True)
            else:
                m_a = jnp.max(s_a, axis=-1, keepdims=True)
                m_b = jnp.max(s_b, axis=-1, keepdims=True)
            p_a = jnp.exp(s_a - m_a).astype(BF16)
            p_b = jnp.exp(s_b - m_b).astype(BF16)
            rhs_a = jnp.concatenate([jnp.where(head_a, v, zero), ones_a], axis=1)
            rhs_b = jnp.concatenate([jnp.where(head_a, zero, v), ones_b], axis=1)
            acc = (jnp.dot(p_a, rhs_a, preferred_element_type=F32)
                   + jnp.dot(p_b, rhs_b, preferred_element_type=F32))
            l = acc[:, pair_w:]
            o_ref[:, ps] = (acc[:, :pair_w] / l).astype(o_ref.dtype)
            lse_pair = jnp.where(lane_q < ATT_HEAD_DIM, m_a, m_b) + jnp.log(l)
            in_pair = _floor_to_pow2(_mod_pow2(lane_q, ATT_HEAD_DIM), LSE_LANES_PER_HEAD) == p * LSE_LANES_PER_HEAD
            lse_parts.append(jnp.where(in_pair, lse_pair, 0.0))
        while len(lse_parts) > 1:
            lse_parts = [a + b for a, b in zip(lse_parts[::2], lse_parts[1::2])]
        lse_ref[...] = lse_parts[0]
        kprev[...] = kc_ref[...]
        vprev[...] = vc_ref[...]

    @pl.when(n > 0)
    def _():
        run(True)

    @pl.when(n == 0)
    def _():
        run(False)


def _dilated_branch(qkv, bias, g, dilation):
    s = qkv.shape[0]
    hd = ATT_HEADS * ATT_HEAD_DIM
    sub_len = s // dilation
    n_blk = sub_len // ATT_BLOCK
    assert s % dilation == 0 and sub_len % ATT_BLOCK == 0

    def cur(j):
        return pl.BlockSpec((ATT_BLOCK, hd), lambda r, n: (r * n_blk + n, j))

    return pl.pallas_call(
        _attn_kernel,
        grid=(dilation, n_blk),
        in_specs=[cur(0), cur(1), cur(2),
                  pl.BlockSpec((1, ATT_HEADS, ATT_BLOCK, 2 * ATT_BLOCK), lambda r, n: (g, 0, 0, 0))],
        out_specs=[pl.BlockSpec((ATT_BLOCK, hd), lambda r, n: (r * n_blk + n, 0)),
                   pl.BlockSpec((ATT_BLOCK, LANES), lambda r, n: (r * n_blk + n, 0))],
        out_shape=[jax.ShapeDtypeStruct((s, hd), BF16), jax.ShapeDtypeStruct((s, LANES), F32)],
        scratch_shapes=[pltpu.VMEM((ATT_BLOCK, hd), BF16), pltpu.VMEM((ATT_BLOCK, hd), BF16)],
        compiler_params=_cparams("arbitrary", "arbitrary"),
        name=f"dilated_attn_g{g}",
    )(qkv, qkv, qkv, bias)


def _attn_combine_kernel(o0, o1, o2, l0, l1, l2, out_ref):
    a0, a1, a2 = l0[...], l1[...], l2[...]
    m = jnp.maximum(jnp.maximum(a0, a1), a2)
    e0, e1, e2 = jnp.exp(a0 - m), jnp.exp(a1 - m), jnp.exp(a2 - m)
    inv = 1.0 / (e0 + e1 + e2)
    hd = out_ref.shape[1]
    r = lax.broadcasted_iota(jnp.int32, (LANES, hd), 0)
    c = lax.broadcasted_iota(jnp.int32, (LANES, hd), 1)
    head = lax.shift_right_logical(c, int(math.log2(ATT_HEAD_DIM)))
    widen = jnp.where(r == _lse_lane(head), 1.0, 0.0)
    widen3 = jnp.concatenate([widen.astype(BF16)] * 3, axis=0)
    acc = jnp.zeros(out_ref.shape, F32)
    for e, o in ((e0, o0), (e1, o1), (e2, o2)):
        acc = acc + _dot_01_right(e * inv, widen3) * o[...].astype(F32)
    out_ref[...] = acc.astype(out_ref.dtype)


def _attn_combine(outs, lses, tm=512):
    s, d = outs[0].shape
    spec = pl.BlockSpec((tm, d), lambda i: (i, 0))
    lspec = pl.BlockSpec((tm, LANES), lambda i: (i, 0))
    return pl.pallas_call(
        _attn_combine_kernel,
        grid=(s // tm,),
        in_specs=[spec] * 3 + [lspec] * 3,
        out_specs=spec,
        out_shape=jax.ShapeDtypeStruct((s, d), BF16),
        compiler_params=_cparams("arbitrary"),
        name="attn_combine",
    )(*outs, *lses)


def _to_strided_order(x, dilation):
    s, c = x.shape
    return x.reshape(s // dilation, dilation, c).transpose(1, 0, 2).reshape(s, c)


def _from_strided_order(x, dilation):
    s, c = x.shape
    return x.reshape(dilation, s // dilation, c).transpose(1, 0, 2).reshape(s, c)


def _dilated_attention(h_bf, w_in_bf, bias):
    per_group = 3 * ATT_HEADS * ATT_HEAD_DIM
    outs, lses = [], []
    for g, (_, dilation) in enumerate(DILATION_CONFIGS):
        qkv = _matmul(_to_strided_order(h_bf, dilation), w_in_bf[:, g * per_group:(g + 1) * per_group], BF16)
        o, l = _dilated_branch(qkv, bias, g, dilation)
        outs.append(_from_strided_order(o, dilation))
        lses.append(_from_strided_order(l, dilation))
    return _attn_combine(outs, lses)


def _head_expand(n_heads, width):
    r = lax.broadcasted_iota(jnp.int32, (LANES, n_heads * width), 0)
    c = lax.broadcasted_iota(jnp.int32, (LANES, n_heads * width), 1)
    return jnp.where(_floor_to_pow2(c, width) == r * width, 1.0, 0.0).astype(BF16)


def _causal_conv_silu(cur, halo, w, b):
    rows = lax.broadcasted_iota(jnp.int32, (SUBLANES, cur.shape[1]), 0)
    acc = cur * w[SSM_CONV - 1:SSM_CONV, :] + b
    for j in range(1, SSM_CONV):
        shifted = pltpu.roll(cur, j, 0)
        head = jnp.where(rows < j, pltpu.roll(halo, j, 0), shifted[:SUBLANES])
        shifted = jnp.concatenate([head, shifted[SUBLANES:]], axis=0)
        acc = acc + shifted * w[SSM_CONV - 1 - j:SSM_CONV - j, :]
    return _silu(acc)


def _ssd_kernel(z_ref, x_ref, bc_ref, xh_ref, bch_ref, dt_ref, cwx_ref, cbx_ref, cwbc_ref, cbbc_ref,
                dtb_ref, aneg_ref, dskip_ref, nw_ref, o_ref, state_ref):
    c = pl.program_id(0)

    @pl.when(c == 0)
    def _():
        state_ref[...] = jnp.zeros_like(state_ref)

    first = c == 0
    xh = jnp.where(first, 0.0, xh_ref[...].astype(F32))
    bch = jnp.where(first, 0.0, bch_ref[...].astype(F32))
    xs = _causal_conv_silu(x_ref[...].astype(F32), xh, cwx_ref[...], cbx_ref[...])
    bc = _causal_conv_silu(bc_ref[...].astype(F32), bch, cwbc_ref[...], cbbc_ref[...])
    gn = SSM_GROUPS * SSM_STATE

    x_dt = dt_ref[...] + dtb_ref[...]
    dt = jnp.maximum(x_dt, 0.0) + jnp.log1p(jnp.exp(-jnp.abs(x_dt)))
    a = dt * aneg_ref[...]
    row = lax.broadcasted_iota(jnp.int32, (SSM_CHUNK, SSM_CHUNK), 0)
    colv = lax.broadcasted_iota(jnp.int32, (SSM_CHUNK, SSM_CHUNK), 1)
    causal = colv <= row
    tri = jnp.where(causal, 1.0, 0.0).astype(BF16)
    a_cs = _dot_01_left(jnp.concatenate([tri] * 3, axis=1), a)
    a_cs_t = a_cs.T
    expand = _head_expand(SSM_HEADS, SSM_HEAD_DIM)
    a_last = a_cs[SSM_CHUNK - 1:SSM_CHUNK, :]
    per_head = jnp.concatenate([dt, jnp.exp(a_cs), jnp.exp(a_last - a_cs)], axis=0)
    per_head_e = _dot_01_right(per_head, jnp.concatenate([expand] * 3, axis=0))
    dt_e = per_head_e[:SSM_CHUNK]
    in_decay_e = per_head_e[SSM_CHUNK:2 * SSM_CHUNK]
    out_decay_e = per_head_e[2 * SSM_CHUNK:]
    chunk_decay_e = in_decay_e[SSM_CHUNK - 1:SSM_CHUNK, :]

    xdt = xs * dt_e
    xdt_bf = xdt.astype(BF16)
    xend_bf = (xdt * out_decay_e).astype(BF16)
    gw = SSM_HEADS_PER_GROUP * SSM_HEAD_DIM
    ys = []
    for g in range(SSM_GROUPS):
        b_g = bc[:, g * SSM_STATE:(g + 1) * SSM_STATE]
        c_g = bc[:, gn + g * SSM_STATE:gn + (g + 1) * SSM_STATE].astype(BF16)
        cb = lax.dot_general(c_g, b_g.astype(BF16), (((1,), (1,)), ((), ())), preferred_element_type=F32)
        prev = state_ref[g]
        y_off = jnp.dot(c_g, prev.astype(BF16), preferred_element_type=F32) * in_decay_e[:, g * gw:(g + 1) * gw]
        y_parts = []
        for r in range(SSM_HEADS_PER_GROUP):
            hh = g * SSM_HEADS_PER_GROUP + r
            seg = jnp.exp(jnp.where(causal, a_cs[:, hh:hh + 1] - a_cs_t[hh:hh + 1, :], -jnp.inf))
            y_parts.append(jnp.dot((cb * seg).astype(BF16), xdt_bf[:, hh * SSM_HEAD_DIM:(hh + 1) * SSM_HEAD_DIM],
                                   preferred_element_type=F32))
        ys.append(jnp.concatenate(y_parts, axis=1) + y_off)
        new = jnp.dot(b_g.T.astype(BF16), xend_bf[:, g * gw:(g + 1) * gw], preferred_element_type=F32)
        state_ref[g] = prev * chunk_decay_e[:, g * gw:(g + 1) * gw] + new
    y = jnp.concatenate(ys, axis=1) + xs * dskip_ref[...]
    u = y * _silu(z_ref[...].astype(F32))
    gsz = SSM_D_INNER // SSM_GROUPS
    outs = []
    for g in range(SSM_GROUPS):
        ug = u[:, g * gsz:(g + 1) * gsz]
        outs.append(ug * lax.rsqrt(jnp.mean(ug * ug, axis=-1, keepdims=True) + NORM_EPS))
    o_ref[...] = (jnp.concatenate(outs, axis=1) * nw_ref[...]).astype(o_ref.dtype)


def _mamba2_inner(zxbc, dt_raw, conv_w, conv_b, dt_bias, a_log, d_skip, norm_w):
    s = zxbc.shape[0]
    n_c = s // SSM_CHUNK
    di = SSM_D_INNER
    halo_blocks = SSM_CHUNK // SUBLANES

    def pad_lanes(v):
        return jnp.pad(v.astype(F32), (0, LANES - v.shape[0])).reshape(1, LANES)

    cw = conv_w.astype(F32)
    cb = conv_b.astype(F32).reshape(1, -1)
    dskip = jnp.repeat(d_skip.astype(F32), SSM_HEAD_DIM).reshape(1, di)
    a_neg = pad_lanes(-jnp.exp(a_log.astype(F32)))

    def halo(colblk):
        return pl.BlockSpec((SUBLANES, di), lambda c: (jnp.maximum(c * halo_blocks - 1, 0), colblk))

    full = lambda shape: pl.BlockSpec(shape, lambda c: (0, 0))
    return pl.pallas_call(
        _ssd_kernel,
        grid=(n_c,),
        in_specs=[pl.BlockSpec((SSM_CHUNK, di), lambda c: (c, 0)),
                  pl.BlockSpec((SSM_CHUNK, di), lambda c: (c, 1)),
                  pl.BlockSpec((SSM_CHUNK, SSM_BC_DIM), lambda c: (c, 2)),
                  halo(1), halo(2),
                  pl.BlockSpec((SSM_CHUNK, LANES), lambda c: (c, 0)),
                  full((SSM_CONV, di)), full((1, di)), full((SSM_CONV, SSM_BC_DIM)), full((1, SSM_BC_DIM)),
                  full((1, LANES)), full((1, LANES)), full((1, di)), full((1, di))],
        out_specs=pl.BlockSpec((SSM_CHUNK, di), lambda c: (c, 0)),
        out_shape=jax.ShapeDtypeStruct((s, di), BF16),
        scratch_shapes=[pltpu.VMEM((SSM_GROUPS, SSM_STATE, SSM_HEADS_PER_GROUP * SSM_HEAD_DIM), F32)],
        compiler_params=_cparams("arbitrary"),
        name="ssd_chunk_scan",
    )(zxbc, zxbc, zxbc, zxbc, zxbc, dt_raw, cw[:, :di], cb[:, :di], cw[:, di:], cb[:, di:],
      pad_lanes(dt_bias), a_neg, dskip, norm_w.astype(F32).reshape(1, di))


def _mamba2_mixer(h_bf, w_in, conv_w, conv_b, dt_bias, a_log, d_skip, norm_w):
    n_main = SSM_D_INNER + SSM_D_INNER + SSM_BC_DIM
    assert SSM_BC_DIM == SSM_D_INNER
    w_main = w_in[:, :n_main].astype(BF16)
    w_dt = jnp.pad(w_in[:, n_main:], ((0, 0), (0, LANES - SSM_HEADS))).astype(BF16)
    zxbc = _matmul(h_bf, w_main, BF16)
    dt_raw = _matmul(h_bf, w_dt, F32)
    return _mamba2_inner(zxbc, dt_raw, conv_w, conv_b, dt_bias, a_log, d_skip, norm_w)


def _gla_kernel(q_ref, f_ref, v_ref, gate_ref, lb_ref, nw_ref, o_ref, state_ref):
    c = pl.program_id(0)

    @pl.when(c == 0)
    def _():
        state_ref[...] = jnp.zeros_like(state_ref)

    L = GLA_CHUNK
    K = HGRN_HEAD_DIM
    n_sub = L // GLA_SUB
    row = lax.broadcasted_iota(jnp.int32, (L, L), 0)
    col = lax.broadcasted_iota(jnp.int32, (L, L), 1)
    tri = jnp.where(col <= row, 1.0, 0.0).astype(BF16)
    tri3 = jnp.concatenate([tri] * 3, axis=1)
    sub3 = lax.broadcasted_iota(jnp.int32, (n_sub, GLA_SUB, L), 1)
    rel3 = (lax.broadcasted_iota(jnp.int32, (n_sub, GLA_SUB, L), 2)
            - lax.broadcasted_iota(jnp.int32, (n_sub, GLA_SUB, L), 0) * GLA_SUB)
    rel3 = jnp.where(rel3 >= 0, jnp.where(rel3 <= sub3, rel3, -1), -1)
    levels = []
    m = 2 * GLA_SUB
    while m <= L:
        levels.append(m)
        m *= 2
    same_block = {m: _floor_to_pow2(row, m) == _floor_to_pow2(col, m) for m in levels if m < L}

    for h in range(HGRN_HEADS):
        hs = slice(h * K, (h + 1) * K)
        f = f_ref[:, hs]
        lb = lb_ref[:, hs]
        q = _silu(q_ref[:, hs].astype(F32))
        v_bf = v_ref[:, hs]
        l1p = jnp.log1p(jnp.exp(-jnp.abs(f)))
        log_k = jnp.log1p(-lb) + jnp.minimum(-f, 0.0) - l1p
        ta = jnp.log(lb)
        tb = jnp.log1p(-lb) + jnp.minimum(f, 0.0) - l1p
        log_f = jnp.maximum(ta, tb) + jnp.log1p(jnp.exp(-jnp.abs(ta - tb)))
        g = _dot_01_left(tri3, log_f)
        gk = g - log_k

        g3 = g.reshape(n_sub, GLA_SUB, K)
        q3 = q.reshape(n_sub, GLA_SUB, K)
        gk3 = gk.reshape(n_sub, GLA_SUB, K)
        a3 = jnp.zeros((n_sub, GLA_SUB, L), F32)
        for s in range(GLA_SUB):
            e = jnp.exp(jnp.minimum(g3 - gk3[:, s:s + 1, :], 0.0))
            val = jnp.sum(q3 * e, axis=-1, keepdims=True)
            a3 = jnp.where(rel3 == s, val, a3)
        att = a3.reshape(L, L)

        for m in levels:
            half = m // 2
            zeros = jnp.zeros((half, K), F32)
            q_parts, k_parts = [], []
            for j in range(L // m):
                lo, mid, hi = j * m, j * m + half, (j + 1) * m
                gb = g[mid - 1:mid, :]
                q_parts += [zeros, q[mid:hi] * jnp.exp(g[mid:hi] - gb)]
                k_parts += [jnp.exp(gb - gk[lo:mid]), zeros]
            ql = jnp.concatenate(q_parts, axis=0).astype(BF16)
            kl = jnp.concatenate(k_parts, axis=0).astype(BF16)
            al = lax.dot_general(ql, kl, (((1,), (1,)), ((), ())), preferred_element_type=F32)
            att = att + (jnp.where(same_block[m], al, 0.0) if m < L else al)

        state_t = state_ref[h]
        o = (jnp.dot(att.astype(BF16), v_bf, preferred_element_type=F32)
             + lax.dot_general((q * jnp.exp(g)).astype(BF16), state_t.astype(BF16), (((1,), (1,)), ((), ())),
                               preferred_element_type=F32))
        g_last = g[L - 1:L, :]
        k_dec = jnp.exp(g_last - gk).astype(BF16)
        v_t = v_bf.astype(F32).T.astype(BF16)
        state_ref[h] = state_t * jnp.exp(g_last) + jnp.dot(v_t, k_dec, preferred_element_type=F32)
        o = o * lax.rsqrt(jnp.mean(o * o, axis=-1, keepdims=True) + NORM_EPS) * nw_ref[...]
        o_ref[:, hs] = (o * _silu(gate_ref[:, hs].astype(F32))).astype(o_ref.dtype)


def _gla(qig, f, lb, norm_w):
    s, d = f.shape
    n_c = s // GLA_CHUNK
    return pl.pallas_call(
        _gla_kernel,
        grid=(n_c,),
        in_specs=[pl.BlockSpec((GLA_CHUNK, d), lambda c: (c, 0)),
                  pl.BlockSpec((GLA_CHUNK, d), lambda c: (c, 0)),
                  pl.BlockSpec((GLA_CHUNK, d), lambda c: (c, 1)),
                  pl.BlockSpec((GLA_CHUNK, d), lambda c: (c, 2)),
                  pl.BlockSpec((1, d), lambda c: (0, 0)),
                  pl.BlockSpec((1, HGRN_HEAD_DIM), lambda c: (0, 0))],
        out_specs=pl.BlockSpec((GLA_CHUNK, d), lambda c: (c, 0)),
        out_shape=jax.ShapeDtypeStruct((s, d), BF16),
        scratch_shapes=[pltpu.VMEM((HGRN_HEADS, HGRN_HEAD_DIM, HGRN_HEAD_DIM), F32)],
        compiler_params=_cparams("arbitrary"),
        name="gla_chunk_scan",
    )(qig, f, qig, qig, lb, norm_w)


def _hgrn2_mixer(h_bf, w_in, lb, norm_w):
    d = D_MODEL
    w_q, w_f, w_i, w_g = (w_in[:, j * d:(j + 1) * d] for j in range(4))
    qig = _matmul(h_bf, jnp.concatenate([w_q, w_i, w_g], axis=1).astype(BF16), BF16)
    f = _matmul(h_bf, w_f.astype(BF16), F32)
    return _gla(qig, f, lb.astype(F32).reshape(1, d), norm_w.astype(F32).reshape(1, HGRN_HEAD_DIM))


def kernel(x, attn_w_in, attn_w_out, rel_bias, ssm_w_in, ssm_conv_w, ssm_conv_b, ssm_dt_bias, ssm_a_log, ssm_d, ssm_norm_w, ssm_w_out, hgrn_w_in, hgrn_lower_bound, hgrn_norm_w, hgrn_w_out, moe_w_coarse, moe_w_fine, moe_w_gate, moe_w_up, moe_w_down, ln_gamma, ln_beta):
    b_, s_, d_ = x.shape
    assert b_ == 1 and d_ == D_MODEL
    lbs = jax.nn.softmax(hgrn_lower_bound.astype(F32), axis=0)
    lbs = jnp.cumsum(lbs, axis=0) - lbs[0]
    bias = _attention_bias(rel_bias)
    h = x.reshape(s_, d_).astype(F32)
    h_bf = h.astype(BF16)
    for layer in range(DEPTH):
        kind = layer % N_MIXERS
        j = layer // N_MIXERS
        if kind == 0:
            a = _dilated_attention(h_bf, attn_w_in[j].astype(BF16), bias)
            w_out = attn_w_out[j]
        elif kind == 1:
            a = _mamba2_mixer(h_bf, ssm_w_in[j], ssm_conv_w[j], ssm_conv_b[j], ssm_dt_bias[j], ssm_a_log[j],
                              ssm_d[j], ssm_norm_w[j])
            w_out = ssm_w_out[j]
        else:
            a = _hgrn2_mixer(h_bf, hgrn_w_in[j], lbs[layer], hgrn_norm_w[j])
            w_out = hgrn_w_out[j]
        gam = ln_gamma[layer].astype(F32)
        bet = ln_beta[layer].astype(F32)
        w_route = _router_weights(moe_w_coarse[layer], moe_w_fine[layer])
        h, route = _mix_ln_route(a, w_out.astype(BF16), h, gam[0:1], bet[0:1], w_route)
        h, h_bf = _moe_layer(h, route, moe_w_gate.astype(F32), moe_w_up.astype(F32), moe_w_down.astype(F32), layer,
                             gam[1:2], bet[1:2])
    return h.reshape(b_, s_, d_)
```

```python
import functools
import math

import jax
import jax.numpy as jnp
from jax import lax
from jax.experimental import pallas as pl
from jax.experimental.pallas import tpu as pltpu

F32 = jnp.float32
BF16 = jnp.bfloat16
HIGHEST = lax.Precision.HIGHEST

D_MODEL = 1024
DEPTH = 4
N_MIXERS = 3
DEEPNORM_ALPHA = (2 * DEPTH) ** 0.25
NORM_EPS = 1e-5
LOG2_E = 1.4426950408889634

ATT_HEAD_DIM = 64
ATT_HEADS = D_MODEL // ATT_HEAD_DIM
DILATION_CONFIGS = ((128, 1), (512, 4), (2048, 16))
N_DIL_GROUPS = len(DILATION_CONFIGS)
ATT_BLOCK = 128
REL_BUCKETS = 32
REL_MAX_DIST = 2048

SSM_D_INNER = 2 * D_MODEL
SSM_HEAD_DIM = 64
SSM_HEADS = SSM_D_INNER // SSM_HEAD_DIM
SSM_GROUPS = 8
SSM_HEADS_PER_GROUP = SSM_HEADS // SSM_GROUPS
SSM_STATE = 128
SSM_CONV = 4
SSM_CHUNK = 128
SSM_BC_DIM = 2 * SSM_GROUPS * SSM_STATE

HGRN_HEAD_DIM = 128
HGRN_HEADS = D_MODEL // HGRN_HEAD_DIM
GLA_CHUNK = 128
GLA_SUB = 8

MOE_GROUPS = 8
MOE_EXPERTS_PER_GROUP = 8
MOE_EXPERTS = MOE_GROUPS * MOE_EXPERTS_PER_GROUP
MOE_TOP_K = 2
MOE_D_FF = 512
MOE_BLOCK = 256

LANES = 128
SUBLANES = 8
VMEM_LIMIT_BYTES = 52 * 1024 * 1024


def _cparams(*sem):
    return pltpu.CompilerParams(dimension_semantics=sem, vmem_limit_bytes=VMEM_LIMIT_BYTES)


def _silu(x):
    hx = 0.5 * x
    return hx + hx * jnp.tanh(hx)


def _split_bf16(v, parts):
    out = []
    for _ in range(parts - 1):
        p = v.astype(BF16)
        out.append(p)
        v = v - p.astype(F32)
    out.append(v.astype(BF16))
    return out


def _dot_01_left(sel3, v):
    return jnp.dot(sel3, jnp.concatenate(_split_bf16(v, 3), axis=0), preferred_element_type=F32)


def _dot_01_right(v, sel3):
    return jnp.dot(jnp.concatenate(_split_bf16(v, 3), axis=1), sel3, preferred_element_type=F32)


def _floor_to_pow2(v, m):
    assert m & (m - 1) == 0
    return jnp.bitwise_and(v, -m)


def _mod_pow2(v, m):
    assert m & (m - 1) == 0
    return jnp.bitwise_and(v, m - 1)


def _mm_kernel(x_ref, w_ref, o_ref):
    o_ref[...] = jnp.dot(x_ref[...], w_ref[...], preferred_element_type=F32).astype(o_ref.dtype)


def _matmul(x, w, out_dtype, tm=1024, tn=1024):
    m, k = x.shape
    n = w.shape[1]
    tm = min(tm, m)
    tn = min(tn, n)
    assert m % tm == 0 and n % tn == 0
    return pl.pallas_call(
        _mm_kernel,
        grid=(n // tn, m // tm),
        in_specs=[pl.BlockSpec((tm, k), lambda j, i: (i, 0)),
                  pl.BlockSpec((k, tn), lambda j, i: (0, j))],
        out_specs=pl.BlockSpec((tm, tn), lambda j, i: (i, j)),
        out_shape=jax.ShapeDtypeStruct((m, n), out_dtype),
        compiler_params=_cparams("arbitrary", "arbitrary"),
        name="mm",
    )(x, w)


def _layer_norm(y, gamma, beta):
    mu = jnp.mean(y, axis=-1, keepdims=True)
    yc = y - mu
    var = jnp.mean(yc * yc, axis=-1, keepdims=True)
    return yc * lax.rsqrt(var + NORM_EPS) * gamma + beta


def _mix_ln_route_kernel(a_ref, w_ref, h_ref, gam_ref, bet_ref, wr_ref, hout_ref, route_ref):
    mix = jnp.dot(a_ref[...], w_ref[...], preferred_element_type=F32)
    hn = _layer_norm(DEEPNORM_ALPHA * h_ref[...] + mix, gam_ref[...], bet_ref[...])
    hout_ref[...] = hn
    hn_hi, hn_lo = _split_bf16(hn, 2)
    both = jnp.dot(hn_hi, wr_ref[...], preferred_element_type=F32)
    logits = (both[:, :LANES] + both[:, LANES:]
              + jnp.dot(hn_lo, wr_ref[:, :LANES], preferred_element_type=F32))
    lane = lax.broadcasted_iota(jnp.int32, logits.shape, 1).astype(F32)
    neg = -jnp.inf
    l1 = jnp.where(lane < MOE_GROUPS, logits, neg)
    m1 = jnp.max(l1, axis=-1, keepdims=True)
    grp = jnp.min(jnp.where(l1 == m1, lane, float(LANES)), axis=-1, keepdims=True)
    g1 = 1.0 / jnp.sum(jnp.exp(l1 - m1), axis=-1, keepdims=True)
    lo = MOE_GROUPS + grp * MOE_EXPERTS_PER_GROUP
    l2 = jnp.where(lane >= lo, jnp.where(lane < lo + MOE_EXPERTS_PER_GROUP, logits, neg), neg)
    t1 = jnp.max(l2, axis=-1, keepdims=True)
    i1 = jnp.min(jnp.where(l2 == t1, lane, float(LANES)), axis=-1, keepdims=True)
    l2b = jnp.where(lane == i1, neg, l2)
    t2 = jnp.max(l2b, axis=-1, keepdims=True)
    i2 = jnp.min(jnp.where(l2b == t2, lane, float(LANES)), axis=-1, keepdims=True)
    e21 = jnp.exp(t2 - t1)
    ga = g1 / (1.0 + e21)
    gb = g1 * e21 / (1.0 + e21)
    out = jnp.where(lane == 0, i1 - MOE_GROUPS,
                    jnp.where(lane == 1, i2 - MOE_GROUPS,
                              jnp.where(lane == 2, ga, jnp.where(lane == 3, gb, 0.0))))
    route_ref[...] = out


def _router_weights(w_coarse, w_fine):
    w = jnp.pad(jnp.concatenate([w_coarse, w_fine], axis=1).astype(F32),
                ((0, 0), (0, LANES - MOE_GROUPS - MOE_EXPERTS)))
    hi = w.astype(BF16)
    lo = (w - hi.astype(F32)).astype(BF16)
    return jnp.concatenate([hi, lo], axis=1)


def _mix_ln_route(a, w, h, gamma, beta, w_route, tm=512):
    s, k = a.shape
    d = w.shape[1]
    return pl.pallas_call(
        _mix_ln_route_kernel,
        grid=(s // tm,),
        in_specs=[pl.BlockSpec((tm, k), lambda i: (i, 0)),
                  pl.BlockSpec((k, d), lambda i: (0, 0)),
                  pl.BlockSpec((tm, d), lambda i: (i, 0)),
                  pl.BlockSpec((1, d), lambda i: (0, 0)),
                  pl.BlockSpec((1, d), lambda i: (0, 0)),
                  pl.BlockSpec((d, 2 * LANES), lambda i: (0, 0))],
        out_specs=[pl.BlockSpec((tm, d), lambda i: (i, 0)),
                   pl.BlockSpec((tm, LANES), lambda i: (i, 0))],
        out_shape=[jax.ShapeDtypeStruct((s, d), F32),
                   jax.ShapeDtypeStruct((s, LANES), F32)],
        compiler_params=_cparams("arbitrary"),
        name="mix_ln_route",
    )(a, w, h, gamma, beta, w_route)


def _start_row_gather(src_hbm, row_of, dst_vmem, sem):
    for i in range(dst_vmem.shape[0]):
        pltpu.make_async_copy(src_hbm.at[pl.ds(row_of(i), 1), :], dst_vmem.at[pl.ds(i, 1), :], sem).start()


def _wait_row_gather(dst_vmem, sem):
    pltpu.make_async_copy(dst_vmem, dst_vmem, sem).wait()


PAD_ROW_STRIDE = 61
MOE_GATHER_BUFFERS = 4
MOE_WEIGHT_SLOTS = 3


def _moe_kernel(bpos_ref, eseq_ref, tok_ref, cnt_ref, h_hbm, wg_hbm, wu_hbm, wd_hbm, y_ref,
                xbuf0, xbuf1, xbuf2, xbuf3, wg_f, wu_f, wd_f, wg_bf, wu_bf, wd_bf, gsems, wsems, *, layer):
    b = pl.program_id(0)
    n_used = cnt_ref[0]
    n_exp = cnt_ref[1]
    bufs = (xbuf0, xbuf1, xbuf2, xbuf3)
    depth = MOE_GATHER_BUFFERS - 1

    def start_gather(blk, r):
        _start_row_gather(h_hbm, lambda i: tok_ref[blk * MOE_BLOCK + i], bufs[r], gsems.at[r])

    def weight_copies(p):
        e = eseq_ref[p]
        slot = lax.rem(p, MOE_WEIGHT_SLOTS)
        return [pltpu.make_async_copy(w.at[layer, e], f.at[slot], wsems.at[slot])
                for w, f in ((wg_hbm, wg_f), (wu_hbm, wu_f), (wd_hbm, wd_f))]

    @pl.when(b == 0)
    def _():
        for r in range(depth):
            start_gather(jnp.minimum(r, n_used - 1), r)
        for c in weight_copies(0):
            c.start()

    @pl.when(jnp.logical_and(b == 0, n_exp > 1))
    def _():
        for c in weight_copies(1):
            c.start()

    p = bpos_ref[b]
    changed = jnp.logical_or(b == 0, p != bpos_ref[jnp.maximum(b - 1, 0)])

    @pl.when(jnp.logical_and(changed, p + 2 < n_exp))
    def _():
        for c in weight_copies(p + 2):
            c.start()

    @pl.when(changed)
    def _():
        for c in weight_copies(p):
            c.wait()
        slot = lax.rem(p, MOE_WEIGHT_SLOTS)
        wg_bf[...] = wg_f[slot].astype(BF16)
        wu_bf[...] = wu_f[slot].astype(BF16)
        wd_bf[...] = wd_f[slot].astype(BF16)

    for r in range(MOE_GATHER_BUFFERS):
        @pl.when(jnp.logical_and(b < n_used, lax.rem(b, MOE_GATHER_BUFFERS) == r))
        def _():
            _wait_row_gather(bufs[r], gsems.at[r])
            start_gather(jnp.minimum(b + depth, n_used - 1), (r + depth) % MOE_GATHER_BUFFERS)
            x = bufs[r][...].astype(BF16)
            g = jnp.dot(x, wg_bf[...], preferred_element_type=F32)
            u = jnp.dot(x, wu_bf[...], preferred_element_type=F32)
            hid = (_silu(g) * u).astype(BF16)
            y_ref[...] = jnp.dot(hid, wd_bf[...], preferred_element_type=F32)

        @pl.when(jnp.logical_and(b == n_used - 1, lax.rem(b, MOE_GATHER_BUFFERS) == r))
        def _():
            for ahead in range(1, MOE_GATHER_BUFFERS):
                rr = (r + ahead) % MOE_GATHER_BUFFERS
                _wait_row_gather(bufs[rr], gsems.at[rr])

    @pl.when(b >= n_used)
    def _():
        y_ref[...] = jnp.zeros_like(y_ref)


def _moe_experts(h, block_pos, expert_seq, slot_token, counts, w_gate, w_up, w_down, layer):
    s, d = h.shape
    n_slots = slot_token.shape[0]
    n_blocks = n_slots // MOE_BLOCK
    dff = w_gate.shape[3]
    hbm = pl.BlockSpec(memory_space=pl.ANY)
    grid_spec = pltpu.PrefetchScalarGridSpec(
        num_scalar_prefetch=4,
        grid=(n_blocks,),
        in_specs=[hbm, hbm, hbm, hbm],
        out_specs=pl.BlockSpec((MOE_BLOCK, d), lambda b, *_: (b, 0)),
        scratch_shapes=([pltpu.VMEM((MOE_BLOCK, d), F32)] * MOE_GATHER_BUFFERS
                        + [pltpu.VMEM((MOE_WEIGHT_SLOTS, d, dff), F32), pltpu.VMEM((MOE_WEIGHT_SLOTS, d, dff), F32),
                           pltpu.VMEM((MOE_WEIGHT_SLOTS, dff, d), F32),
                           pltpu.VMEM((d, dff), BF16), pltpu.VMEM((d, dff), BF16), pltpu.VMEM((dff, d), BF16),
                           pltpu.SemaphoreType.DMA((MOE_GATHER_BUFFERS,)),
                           pltpu.SemaphoreType.DMA((MOE_WEIGHT_SLOTS,))]),
    )
    return pl.pallas_call(
        functools.partial(_moe_kernel, layer=layer),
        grid_spec=grid_spec,
        out_shape=jax.ShapeDtypeStruct((n_slots, d), F32),
        compiler_params=_cparams("arbitrary"),
        name="moe_experts",
    )(block_pos, expert_seq, slot_token, counts, h, w_gate, w_up, w_down)


COMBINE_TILE = 256


COMBINE_BUFFERS = 3


def _combine_ln_kernel(pos_ref, y_hbm, h_ref, route_ref, gam_ref, bet_ref, hout_ref, hbf_ref, *scratch):
    t = pl.program_id(0)
    n_t = pl.num_programs(0)
    sems = scratch[-1]
    bufs = [scratch[r * MOE_TOP_K:(r + 1) * MOE_TOP_K] for r in range(COMBINE_BUFFERS)]
    depth = COMBINE_BUFFERS - 1

    def start_gather(tile, r):
        for k in range(MOE_TOP_K):
            _start_row_gather(y_hbm, lambda i: pos_ref[(tile * COMBINE_TILE + i) * MOE_TOP_K + k],
                              bufs[r][k], sems.at[r])

    def wait_gather(r):
        for k in range(MOE_TOP_K):
            _wait_row_gather(bufs[r][k], sems.at[r])

    @pl.when(t == 0)
    def _():
        for r in range(depth):
            start_gather(jnp.minimum(r, n_t - 1), r)

    for r in range(COMBINE_BUFFERS):
        @pl.when(lax.rem(t, COMBINE_BUFFERS) == r)
        def _():
            wait_gather(r)
            start_gather(jnp.minimum(t + depth, n_t - 1), (r + depth) % COMBINE_BUFFERS)
            route = route_ref[...]
            ffn = route[:, 2:3] * bufs[r][0][...] + route[:, 3:4] * bufs[r][1][...]
            hn = _layer_norm(DEEPNORM_ALPHA * h_ref[...] + ffn, gam_ref[...], bet_ref[...])
            hout_ref[...] = hn
            hbf_ref[...] = hn.astype(BF16)

        @pl.when(jnp.logical_and(t == n_t - 1, lax.rem(t, COMBINE_BUFFERS) == r))
        def _():
            for ahead in range(1, COMBINE_BUFFERS):
                wait_gather((r + ahead) % COMBINE_BUFFERS)


def _combine_ln(y_slots, pos, h, route, gamma, beta):
    s, d = h.shape
    tm = COMBINE_TILE
    grid_spec = pltpu.PrefetchScalarGridSpec(
        num_scalar_prefetch=1,
        grid=(s // tm,),
        in_specs=[pl.BlockSpec(memory_space=pl.ANY),
                  pl.BlockSpec((tm, d), lambda i, pos: (i, 0)),
                  pl.BlockSpec((tm, LANES), lambda i, pos: (i, 0)),
                  pl.BlockSpec((1, d), lambda i, pos: (0, 0)),
                  pl.BlockSpec((1, d), lambda i, pos: (0, 0))],
        out_specs=[pl.BlockSpec((tm, d), lambda i, pos: (i, 0)),
                   pl.BlockSpec((tm, d), lambda i, pos: (i, 0))],
        scratch_shapes=([pltpu.VMEM((tm, d), F32)] * (COMBINE_BUFFERS * MOE_TOP_K)
                        + [pltpu.SemaphoreType.DMA((COMBINE_BUFFERS,))]),
    )
    return pl.pallas_call(
        _combine_ln_kernel,
        grid_spec=grid_spec,
        out_shape=[jax.ShapeDtypeStruct((s, d), F32), jax.ShapeDtypeStruct((s, d), BF16)],
        compiler_params=_cparams("arbitrary"),
        name="combine_ln",
    )(pos, y_slots, h, route, gamma, beta)


def _dispatch_tables(route):
    s = route.shape[0]
    n_asg = s * MOE_TOP_K
    expert = route[:, :MOE_TOP_K].astype(jnp.int32).reshape(n_asg)
    onehot = (expert[:, None] == jnp.arange(MOE_EXPERTS, dtype=jnp.int32)[None, :]).astype(jnp.int32)
    csum = jnp.cumsum(onehot, axis=0)
    counts = csum[-1]
    padded = (counts + MOE_BLOCK - 1) // MOE_BLOCK * MOE_BLOCK
    pend = jnp.cumsum(padded)
    pstart = pend - padded
    pos = jnp.sum(onehot * (csum - 1 + pstart[None, :]), axis=1).astype(jnp.int32)
    n_blocks = (n_asg + MOE_EXPERTS * (MOE_BLOCK - 1) + MOE_BLOCK - 1) // MOE_BLOCK
    n_slots = n_blocks * MOE_BLOCK
    token_ids = jnp.arange(n_asg, dtype=jnp.int32) // MOE_TOP_K
    filler = (jnp.arange(n_slots, dtype=jnp.int32) * PAD_ROW_STRIDE) % s
    slot_token = filler.at[pos].set(token_ids, unique_indices=True)
    block_start = jnp.arange(n_blocks, dtype=jnp.int32) * MOE_BLOCK
    block_expert = jnp.sum((pend[None, :] <= block_start[:, None]).astype(jnp.int32), axis=1)
    block_expert = jnp.minimum(block_expert, MOE_EXPERTS - 1).astype(jnp.int32)
    present = counts > 0
    seq_pos = jnp.cumsum(present.astype(jnp.int32)) - 1
    n_exp = seq_pos[-1] + 1
    expert_ids = jnp.arange(MOE_EXPERTS, dtype=jnp.int32)
    expert_seq = jnp.zeros((MOE_EXPERTS,), jnp.int32).at[jnp.where(present, seq_pos, MOE_EXPERTS)].set(
        expert_ids, mode='drop')
    block_pos = jnp.sum(jnp.where(block_expert[:, None] == expert_ids[None, :], seq_pos[None, :], 0), axis=1)
    block_pos = jnp.minimum(block_pos, n_exp - 1).astype(jnp.int32)
    counts2 = jnp.stack([pend[-1] // MOE_BLOCK, n_exp]).astype(jnp.int32)
    return pos, slot_token, block_pos, expert_seq, counts2


def _moe_layer(h, route, w_gate, w_up, w_down, layer, gamma, beta):
    pos, slot_token, block_pos, expert_seq, counts = _dispatch_tables(route)
    y_slots = _moe_experts(h, block_pos, expert_seq, slot_token, counts, w_gate, w_up, w_down, layer)
    return _combine_ln(y_slots, pos, h, route, gamma, beta)


def _t5_bucket(dist):
    n = jnp.maximum(dist, 0)
    max_exact = REL_BUCKETS // 2
    ratio = jnp.maximum(n, max_exact).astype(F32) / max_exact
    large = max_exact + (jnp.log(ratio) / math.log(REL_MAX_DIST / max_exact) * (REL_BUCKETS - max_exact)).astype(jnp.int32)
    large = jnp.minimum(large, REL_BUCKETS - 1)
    return jnp.where(n < max_exact, n, large)


def _bias_kernel(idx_ref, tab_ref, o_ref):
    idx = idx_ref[0]
    tab = tab_ref[0]
    acc = jnp.zeros(o_ref.shape[1:], F32)
    for b in range(REL_BUCKETS):
        acc = jnp.where(idx == b, tab[:, b:b + 1], acc)
    p = lax.broadcasted_iota(jnp.int32, acc.shape, 1)
    ik = _mod_pow2(p, 2 * ATT_BLOCK)
    iq = lax.shift_right_logical(p, int(math.log2(2 * ATT_BLOCK)))
    rel = iq + ATT_BLOCK - ik
    o_ref[0] = jnp.where((rel >= 0) & (rel <= ATT_BLOCK), acc, -jnp.inf)


def _attention_bias(rel_bias):
    iq = jnp.arange(ATT_BLOCK)[:, None]
    ik = jnp.arange(2 * ATT_BLOCK)[None, :]
    rel = iq + ATT_BLOCK - ik
    idx = jnp.stack([_t5_bucket(rel * dil) for _, dil in DILATION_CONFIGS], 0)
    n_pairs = ATT_BLOCK * 2 * ATT_BLOCK
    idx = idx.reshape(N_DIL_GROUPS, 1, n_pairs).astype(jnp.int32)
    tab = jnp.transpose(rel_bias.astype(F32), (1, 2, 0))
    bias = pl.pallas_call(
        _bias_kernel,
        grid=(N_DIL_GROUPS,),
        in_specs=[pl.BlockSpec((1, 1, n_pairs), lambda g: (g, 0, 0)),
                  pl.BlockSpec((1, ATT_HEADS, REL_BUCKETS), lambda g: (g, 0, 0))],
        out_specs=pl.BlockSpec((1, ATT_HEADS, n_pairs), lambda g: (g, 0, 0)),
        out_shape=jax.ShapeDtypeStruct((N_DIL_GROUPS, ATT_HEADS, n_pairs), F32),
        compiler_params=_cparams("arbitrary"),
        name="attn_bias",
    )(idx, tab)
    return bias.reshape(N_DIL_GROUPS, ATT_HEADS, ATT_BLOCK, 2 * ATT_BLOCK)


LSE_LANES_PER_HEAD = LANES // ATT_HEADS


def _lse_lane(h):
    return (h & 1) * ATT_HEAD_DIM + (h >> 1) * LSE_LANES_PER_HEAD


def _attn_kernel(q_ref, kc_ref, vc_ref, bias_ref, o_ref, lse_ref, kprev, vprev):
    n = pl.program_id(1)
    pair_w = 2 * ATT_HEAD_DIM
    assert pair_w == LANES and 2 * LSE_LANES_PER_HEAD * (ATT_HEADS // 2) == LANES
    dn = (((1,), (1,)), ((), ()))

    def run(use_prev):
        n_keys = (2 if use_prev else 1) * ATT_BLOCK
        lane_k = lax.broadcasted_iota(jnp.int32, (n_keys, pair_w), 1)
        head_a = lane_k < ATT_HEAD_DIM
        ones_a = jnp.where(head_a, 1.0, 0.0).astype(BF16)
        ones_b = jnp.where(head_a, 0.0, 1.0).astype(BF16)
        lane_q = lax.broadcasted_iota(jnp.int32, (ATT_BLOCK, pair_w), 1)
        zero = jnp.zeros((n_keys, pair_w), BF16)
        lse_parts = []
        for p in range(ATT_HEADS // 2):
            ps = slice(p * pair_w, (p + 1) * pair_w)
            q = q_ref[:, ps] * (ATT_HEAD_DIM ** -0.5)
            if use_prev:
                k = jnp.concatenate([kprev[:, ps], kc_ref[:, ps]], axis=0)
                v = jnp.concatenate([vprev[:, ps], vc_ref[:, ps]], axis=0)
            else:
                k = kc_ref[:, ps]
                v = vc_ref[:, ps]
            k_ab = jnp.concatenate([jnp.where(head_a, k, zero), jnp.where(head_a, zero, k)], axis=0)
            s = lax.dot_general(q, k_ab, dn, preferred_element_type=F32)
            key0 = 0 if use_prev else ATT_BLOCK
            s_a = s[:, :n_keys] + bias_ref[0, 2 * p][:, key0:]
            s_b = s[:, n_keys:] + bias_ref[0, 2 * p + 1][:, key0:]
            if use_prev:
                m_a = jnp.max(jnp.maximum(s_a[:, :ATT_BLOCK], s_a[:, ATT_BLOCK:]), axis=-1, keepdims=True)
                m_b = jnp.max(jnp.maximum(s_b[:, :ATT_BLOCK], s_b[:, ATT_BLOCK:]), axis=-1, keepdims=True)
            else:
                m_a = jnp.max(s_a, axis=-1, keepdims=True)
                m_b = jnp.max(s_b, axis=-1, keepdims=True)
            p_a = jnp.exp(s_a - m_a).astype(BF16)
            p_b = jnp.exp(s_b - m_b).astype(BF16)
            rhs_a = jnp.concatenate([jnp.where(head_a, v, zero), ones_a], axis=1)
            rhs_b = jnp.concatenate([jnp.where(head_a, zero, v), ones_b], axis=1)
            acc = (jnp.dot(p_a, rhs_a, preferred_element_type=F32)
                   + jnp.dot(p_b, rhs_b, preferred_element_type=F32))
            l = acc[:, pair_w:]
            o_ref[:, ps] = (acc[:, :pair_w] / l).astype(o_ref.dtype)
            lse_pair = jnp.where(lane_q < ATT_HEAD_DIM, m_a, m_b) + jnp.log(l)
            in_pair = _floor_to_pow2(_mod_pow2(lane_q, ATT_HEAD_DIM), LSE_LANES_PER_HEAD) == p * LSE_LANES_PER_HEAD
            lse_parts.append(jnp.where(in_pair, lse_pair, 0.0))
        while len(lse_parts) > 1:
            lse_parts = [a + b for a, b in zip(lse_parts[::2], lse_parts[1::2])]
        lse_ref[...] = lse_parts[0]
        kprev[...] = kc_ref[...]
        vprev[...] = vc_ref[...]

    @pl.when(n > 0)
    def _():
        run(True)

    @pl.when(n == 0)
    def _():
        run(False)


def _dilated_branch(qkv, bias, g, dilation):
    s = qkv.shape[0]
    hd = ATT_HEADS * ATT_HEAD_DIM
    sub_len = s // dilation
    n_blk = sub_len // ATT_BLOCK
    assert s % dilation == 0 and sub_len % ATT_BLOCK == 0

    def cur(j):
        return pl.BlockSpec((ATT_BLOCK, hd), lambda r, n: (r * n_blk + n, j))

    return pl.pallas_call(
        _attn_kernel,
        grid=(dilation, n_blk),
        in_specs=[cur(0), cur(1), cur(2),
                  pl.BlockSpec((1, ATT_HEADS, ATT_BLOCK, 2 * ATT_BLOCK), lambda r, n: (g, 0, 0, 0))],
        out_specs=[pl.BlockSpec((ATT_BLOCK, hd), lambda r, n: (r * n_blk + n, 0)),
                   pl.BlockSpec((ATT_BLOCK, LANES), lambda r, n: (r * n_blk + n, 0))],
        out_shape=[jax.ShapeDtypeStruct((s, hd), BF16), jax.ShapeDtypeStruct((s, LANES), F32)],
        scratch_shapes=[pltpu.VMEM((ATT_BLOCK, hd), BF16), pltpu.VMEM((ATT_BLOCK, hd), BF16)],
        compiler_params=_cparams("arbitrary", "arbitrary"),
        name=f"dilated_attn_g{g}",
    )(qkv, qkv, qkv, bias)


def _attn_combine_kernel(o0, o1, o2, l0, l1, l2, out_ref):
    a0, a1, a2 = l0[...], l1[...], l2[...]
    m = jnp.maximum(jnp.maximum(a0, a1), a2)
    e0, e1, e2 = jnp.exp(a0 - m), jnp.exp(a1 - m), jnp.exp(a2 - m)
    inv = 1.0 / (e0 + e1 + e2)
    hd = out_ref.shape[1]
    r = lax.broadcasted_iota(jnp.int32, (LANES, hd), 0)
    c = lax.broadcasted_iota(jnp.int32, (LANES, hd), 1)
    head = lax.shift_right_logical(c, int(math.log2(ATT_HEAD_DIM)))
    widen = jnp.where(r == _lse_lane(head), 1.0, 0.0)
    widen3 = jnp.concatenate([widen.astype(BF16)] * 3, axis=0)
    acc = jnp.zeros(out_ref.shape, F32)
    for e, o in ((e0, o0), (e1, o1), (e2, o2)):
        acc = acc + _dot_01_right(e * inv, widen3) * o[...].astype(F32)
    out_ref[...] = acc.astype(out_ref.dtype)


def _attn_combine(outs, lses, tm=512):
    s, d = outs[0].shape
    spec = pl.BlockSpec((tm, d), lambda i: (i, 0))
    lspec = pl.BlockSpec((tm, LANES), lambda i: (i, 0))
    return pl.pallas_call(
        _attn_combine_kernel,
        grid=(s // tm,),
        in_specs=[spec] * 3 + [lspec] * 3,
        out_specs=spec,
        out_shape=jax.ShapeDtypeStruct((s, d), BF16),
        compiler_params=_cparams("arbitrary"),
        name="attn_combine",
    )(*outs, *lses)


def _to_strided_order(x, dilation):
    s, c = x.shape
    return x.reshape(s // dilation, dilation, c).transpose(1, 0, 2).reshape(s, c)


def _from_strided_order(x, dilation):
    s, c = x.shape
    return x.reshape(dilation, s // dilation, c).transpose(1, 0, 2).reshape(s, c)


def _dilated_attention(h_bf, w_in_bf, bias):
    per_group = 3 * ATT_HEADS * ATT_HEAD_DIM
    outs, lses = [], []
    for g, (_, dilation) in enumerate(DILATION_CONFIGS):
        qkv = _matmul(_to_strided_order(h_bf, dilation), w_in_bf[:, g * per_group:(g + 1) * per_group], BF16)
        o, l = _dilated_branch(qkv, bias, g, dilation)
        outs.append(_from_strided_order(o, dilation))
        lses.append(_from_strided_order(l, dilation))
    return _attn_combine(outs, lses)


def _head_expand(n_heads, width):
    r = lax.broadcasted_iota(jnp.int32, (LANES, n_heads * width), 0)
    c = lax.broadcasted_iota(jnp.int32, (LANES, n_heads * width), 1)
    return jnp.where(_floor_to_pow2(c, width) == r * width, 1.0, 0.0).astype(BF16)


def _causal_conv_silu(cur, halo, w, b):
    rows = lax.broadcasted_iota(jnp.int32, (SUBLANES, cur.shape[1]), 0)
    acc = cur * w[SSM_CONV - 1:SSM_CONV, :] + b
    for j in range(1, SSM_CONV):
        shifted = pltpu.roll(cur, j, 0)
        head = jnp.where(rows < j, pltpu.roll(halo, j, 0), shifted[:SUBLANES])
        shifted = jnp.concatenate([head, shifted[SUBLANES:]], axis=0)
        acc = acc + shifted * w[SSM_CONV - 1 - j:SSM_CONV - j, :]
    return _silu(acc)


def _ssd_kernel(z_ref, x_ref, bc_ref, xh_ref, bch_ref, dt_ref, cwx_ref, cbx_ref, cwbc_ref, cbbc_ref,
                dtb_ref, aneg_ref, dskip_ref, nw_ref, o_ref, state_ref):
    c = pl.program_id(0)

    @pl.when(c == 0)
    def _():
        state_ref[...] = jnp.zeros_like(state_ref)

    first = c == 0
    xh = jnp.where(first, 0.0, xh_ref[...].astype(F32))
    bch = jnp.where(first, 0.0, bch_ref[...].astype(F32))
    xs = _causal_conv_silu(x_ref[...].astype(F32), xh, cwx_ref[...], cbx_ref[...])
    bc = _causal_conv_silu(bc_ref[...].astype(F32), bch, cwbc_ref[...], cbbc_ref[...])
    gn = SSM_GROUPS * SSM_STATE

    x_dt = dt_ref[...] + dtb_ref[...]
    dt = jnp.maximum(x_dt, 0.0) + jnp.log1p(jnp.exp(-jnp.abs(x_dt)))
    a = dt * aneg_ref[...]
    row = lax.broadcasted_iota(jnp.int32, (SSM_CHUNK, SSM_CHUNK), 0)
    colv = lax.broadcasted_iota(jnp.int32, (SSM_CHUNK, SSM_CHUNK), 1)
    causal = colv <= row
    tri = jnp.where(causal, 1.0, 0.0).astype(BF16)
    a_cs = _dot_01_left(jnp.concatenate([tri] * 3, axis=1), a)
    a_cs_t = a_cs.T
    expand = _head_expand(SSM_HEADS, SSM_HEAD_DIM)
    a_last = a_cs[SSM_CHUNK - 1:SSM_CHUNK, :]
    per_head = jnp.concatenate([dt, jnp.exp(a_cs), jnp.exp(a_last - a_cs)], axis=0)
    per_head_e = _dot_01_right(per_head, jnp.concatenate([expand] * 3, axis=0))
    dt_e = per_head_e[:SSM_CHUNK]
    in_decay_e = per_head_e[SSM_CHUNK:2 * SSM_CHUNK]
    out_decay_e = per_head_e[2 * SSM_CHUNK:]
    chunk_decay_e = in_decay_e[SSM_CHUNK - 1:SSM_CHUNK, :]

    xdt = xs * dt_e
    xdt_bf = xdt.astype(BF16)
    xend_bf = (xdt * out_decay_e).astype(BF16)
    gw = SSM_HEADS_PER_GROUP * SSM_HEAD_DIM
    ys = []
    for g in range(SSM_GROUPS):
        b_g = bc[:, g * SSM_STATE:(g + 1) * SSM_STATE]
        c_g = bc[:, gn + g * SSM_STATE:gn + (g + 1) * SSM_STATE].astype(BF16)
        cb = lax.dot_general(c_g, b_g.astype(BF16), (((1,), (1,)), ((), ())), preferred_element_type=F32)
        prev = state_ref[g]
        y_off = jnp.dot(c_g, prev.astype(BF16), preferred_element_type=F32) * in_decay_e[:, g * gw:(g + 1) * gw]
        y_parts = []
        for r in range(SSM_HEADS_PER_GROUP):
            hh = g * SSM_HEADS_PER_GROUP + r
            seg = jnp.exp(jnp.where(causal, a_cs[:, hh:hh + 1] - a_cs_t[hh:hh + 1, :], -jnp.inf))
            y_parts.append(jnp.dot((cb * seg).astype(BF16), xdt_bf[:, hh * SSM_HEAD_DIM:(hh + 1) * SSM_HEAD_DIM],
                                   preferred_element_type=F32))
        ys.append(jnp.concatenate(y_parts, axis=1) + y_off)
        new = jnp.dot(b_g.T.astype(BF16), xend_bf[:, g * gw:(g + 1) * gw], preferred_element_type=F32)
        state_ref[g] = prev * chunk_decay_e[:, g * gw:(g + 1) * gw] + new
    y = jnp.concatenate(ys, axis=1) + xs * dskip_ref[...]
    u = y * _silu(z_ref[...].astype(F32))
    gsz = SSM_D_INNER // SSM_GROUPS
    outs = []
    for g in range(SSM_GROUPS):
        ug = u[:, g * gsz:(g + 1) * gsz]
        outs.append(ug * lax.rsqrt(jnp.mean(ug * ug, axis=-1, keepdims=True) + NORM_EPS))
    o_ref[...] = (jnp.concatenate(outs, axis=1) * nw_ref[...]).astype(o_ref.dtype)


def _mamba2_inner(zxbc, dt_raw, conv_w, conv_b, dt_bias, a_log, d_skip, norm_w):
    s = zxbc.shape[0]
    n_c = s // SSM_CHUNK
    di = SSM_D_INNER
    halo_blocks = SSM_CHUNK // SUBLANES

    def pad_lanes(v):
        return jnp.pad(v.astype(F32), (0, LANES - v.shape[0])).reshape(1, LANES)

    cw = conv_w.astype(F32)
    cb = conv_b.astype(F32).reshape(1, -1)
    dskip = jnp.repeat(d_skip.astype(F32), SSM_HEAD_DIM).reshape(1, di)
    a_neg = pad_lanes(-jnp.exp(a_log.astype(F32)))

    def halo(colblk):
        return pl.BlockSpec((SUBLANES, di), lambda c: (jnp.maximum(c * halo_blocks - 1, 0), colblk))

    full = lambda shape: pl.BlockSpec(shape, lambda c: (0, 0))
    return pl.pallas_call(
        _ssd_kernel,
        grid=(n_c,),
        in_specs=[pl.BlockSpec((SSM_CHUNK, di), lambda c: (c, 0)),
                  pl.BlockSpec((SSM_CHUNK, di), lambda c: (c, 1)),
                  pl.BlockSpec((SSM_CHUNK, SSM_BC_DIM), lambda c: (c, 2)),
                  halo(1), halo(2),
                  pl.BlockSpec((SSM_CHUNK, LANES), lambda c: (c, 0)),
                  full((SSM_CONV, di)), full((1, di)), full((SSM_CONV, SSM_BC_DIM)), full((1, SSM_BC_DIM)),
                  full((1, LANES)), full((1, LANES)), full((1, di)), full((1, di))],
        out_specs=pl.BlockSpec((SSM_CHUNK, di), lambda c: (c, 0)),
        out_shape=jax.ShapeDtypeStruct((s, di), BF16),
        scratch_shapes=[pltpu.VMEM((SSM_GROUPS, SSM_STATE, SSM_HEADS_PER_GROUP * SSM_HEAD_DIM), F32)],
        compiler_params=_cparams("arbitrary"),
        name="ssd_chunk_scan",
    )(zxbc, zxbc, zxbc, zxbc, zxbc, dt_raw, cw[:, :di], cb[:, :di], cw[:, di:], cb[:, di:],
      pad_lanes(dt_bias), a_neg, dskip, norm_w.astype(F32).reshape(1, di))


def _mamba2_mixer(h_bf, w_in, conv_w, conv_b, dt_bias, a_log, d_skip, norm_w):
    n_main = SSM_D_INNER + SSM_D_INNER + SSM_BC_DIM
    assert SSM_BC_DIM == SSM_D_INNER
    w_main = w_in[:, :n_main].astype(BF16)
    w_dt = jnp.pad(w_in[:, n_main:], ((0, 0), (0, LANES - SSM_HEADS))).astype(BF16)
    zxbc = _matmul(h_bf, w_main, BF16)
    dt_raw = _matmul(h_bf, w_dt, F32)
    return _mamba2_inner(zxbc, dt_raw, conv_w, conv_b, dt_bias, a_log, d_skip, norm_w)


def _gla_kernel(q_ref, f_ref, v_ref, gate_ref, lb_ref, nw_ref, o_ref, state_ref):
    c = pl.program_id(0)

    @pl.when(c == 0)
    def _():
        state_ref[...] = jnp.zeros_like(state_ref)

    L = GLA_CHUNK
    K = HGRN_HEAD_DIM
    n_sub = L // GLA_SUB
    row = lax.broadcasted_iota(jnp.int32, (L, L), 0)
    col = lax.broadcasted_iota(jnp.int32, (L, L), 1)
    tri = jnp.where(col <= row, 1.0, 0.0).astype(BF16)
    tri3 = jnp.concatenate([tri] * 3, axis=1)
    sub3 = lax.broadcasted_iota(jnp.int32, (n_sub, GLA_SUB, L), 1)
    rel3 = (lax.broadcasted_iota(jnp.int32, (n_sub, GLA_SUB, L), 2)
            - lax.broadcasted_iota(jnp.int32, (n_sub, GLA_SUB, L), 0) * GLA_SUB)
    rel3 = jnp.where(rel3 >= 0, jnp.where(rel3 <= sub3, rel3, -1), -1)
    levels = []
    m = 2 * GLA_SUB
    while m <= L:
        levels.append(m)
        m *= 2
    same_block = {m: _floor_to_pow2(row, m) == _floor_to_pow2(col, m) for m in levels if m < L}

    for h in range(HGRN_HEADS):
        hs = slice(h * K, (h + 1) * K)
        f = f_ref[:, hs]
        lb = lb_ref[:, hs]
        q = _silu(q_ref[:, hs].astype(F32))
        v_bf = v_ref[:, hs]
        l1p = jnp.log1p(jnp.exp(-jnp.abs(f)))
        log_k = jnp.log1p(-lb) + jnp.minimum(-f, 0.0) - l1p
        ta = jnp.log(lb)
        tb = jnp.log1p(-lb) + jnp.minimum(f, 0.0) - l1p
        log_f = jnp.maximum(ta, tb) + jnp.log1p(jnp.exp(-jnp.abs(ta - tb)))
        g = _dot_01_left(tri3, log_f * LOG2_E)
        gk = g - log_k * LOG2_E

        g3 = g.reshape(n_sub, GLA_SUB, K)
        q3 = q.reshape(n_sub, GLA_SUB, K)
        gk3 = gk.reshape(n_sub, GLA_SUB, K)
        a3 = jnp.zeros((n_sub, GLA_SUB, L), F32)
        for s in range(GLA_SUB):
            e = jnp.exp2(g3 - gk3[:, s:s + 1, :])
            val = jnp.sum(q3 * e, axis=-1, keepdims=True)
            a3 = jnp.where(rel3 == s, val, a3)
        att = a3.reshape(L, L)

        for m in levels:
            half = m // 2
            zeros = jnp.zeros((half, K), F32)
            q_parts, k_parts = [], []
            for j in range(L // m):
                lo, mid, hi = j * m, j * m + half, (j + 1) * m
                gb = g[mid - 1:mid, :]
                q_parts += [zeros, q[mid:hi] * jnp.exp2(g[mid:hi] - gb)]
                k_parts += [jnp.exp2(gb - gk[lo:mid]), zeros]
            ql = jnp.concatenate(q_parts, axis=0).astype(BF16)
            kl = jnp.concatenate(k_parts, axis=0).astype(BF16)
            al = lax.dot_general(ql, kl, (((1,), (1,)), ((), ())), preferred_element_type=F32)
            att = att + (jnp.where(same_block[m], al, 0.0) if m < L else al)

        state_t = state_ref[h]
        o = (jnp.dot(att.astype(BF16), v_bf, preferred_element_type=F32)
             + lax.dot_general((q * jnp.exp2(g)).astype(BF16), state_t.astype(BF16), (((1,), (1,)), ((), ())),
                               preferred_element_type=F32))
        g_last = g[L - 1:L, :]
        k_dec = jnp.exp2(g_last - gk).astype(BF16)
        v_t = v_bf.astype(F32).T.astype(BF16)
        state_ref[h] = state_t * jnp.exp2(g_last) + jnp.dot(v_t, k_dec, preferred_element_type=F32)
        o = o * lax.rsqrt(jnp.mean(o * o, axis=-1, keepdims=True) + NORM_EPS) * nw_ref[...]
        o_ref[:, hs] = (o * _silu(gate_ref[:, hs].astype(F32))).astype(o_ref.dtype)


def _gla(qig, f, lb, norm_w):
    s, d = f.shape
    n_c = s // GLA_CHUNK
    return pl.pallas_call(
        _gla_kernel,
        grid=(n_c,),
        in_specs=[pl.BlockSpec((GLA_CHUNK, d), lambda c: (c, 0)),
                  pl.BlockSpec((GLA_CHUNK, d), lambda c: (c, 0)),
                  pl.BlockSpec((GLA_CHUNK, d), lambda c: (c, 1)),
                  pl.BlockSpec((GLA_CHUNK, d), lambda c: (c, 2)),
                  pl.BlockSpec((1, d), lambda c: (0, 0)),
                  pl.BlockSpec((1, HGRN_HEAD_DIM), lambda c: (0, 0))],
        out_specs=pl.BlockSpec((GLA_CHUNK, d), lambda c: (c, 0)),
        out_shape=jax.ShapeDtypeStruct((s, d), BF16),
        scratch_shapes=[pltpu.VMEM((HGRN_HEADS, HGRN_HEAD_DIM, HGRN_HEAD_DIM), F32)],
        compiler_params=_cparams("arbitrary"),
        name="gla_chunk_scan",
    )(qig, f, qig, qig, lb, norm_w)


def _hgrn2_mixer(h_bf, w_in, lb, norm_w):
    d = D_MODEL
    w_q, w_f, w_i, w_g = (w_in[:, j * d:(j + 1) * d] for j in range(4))
    qig = _matmul(h_bf, jnp.concatenate([w_q, w_i, w_g], axis=1).astype(BF16), BF16)
    f = _matmul(h_bf, w_f.astype(BF16), F32)
    return _gla(qig, f, lb.astype(F32).reshape(1, d), norm_w.astype(F32).reshape(1, HGRN_HEAD_DIM))


def kernel(x, attn_w_in, attn_w_out, rel_bias, ssm_w_in, ssm_conv_w, ssm_conv_b, ssm_dt_bias, ssm_a_log, ssm_d, ssm_norm_w, ssm_w_out, hgrn_w_in, hgrn_lower_bound, hgrn_norm_w, hgrn_w_out, moe_w_coarse, moe_w_fine, moe_w_gate, moe_w_up, moe_w_down, ln_gamma, ln_beta):
    b_, s_, d_ = x.shape
    assert b_ == 1 and d_ == D_MODEL
    lbs = jax.nn.softmax(hgrn_lower_bound.astype(F32), axis=0)
    lbs = jnp.cumsum(lbs, axis=0) - lbs[0]
    bias = _attention_bias(rel_bias)
    h = x.reshape(s_, d_).astype(F32)
    h_bf = h.astype(BF16)
    for layer in range(DEPTH):
        kind = layer % N_MIXERS
        j = layer // N_MIXERS
        if kind == 0:
            a = _dilated_attention(h_bf, attn_w_in[j].astype(BF16), bias)
            w_out = attn_w_out[j]
        elif kind == 1:
            a = _mamba2_mixer(h_bf, ssm_w_in[j], ssm_conv_w[j], ssm_conv_b[j], ssm_dt_bias[j], ssm_a_log[j],
                              ssm_d[j], ssm_norm_w[j])
            w_out = ssm_w_out[j]
        else:
            a = _hgrn2_mixer(h_bf, hgrn_w_in[j], lbs[layer], hgrn_norm_w[j])
            w_out = hgrn_w_out[j]
        gam = ln_gamma[layer].astype(F32)
        bet = ln_beta[layer].astype(F32)
        w_route = _router_weights(moe_w_coarse[layer], moe_w_fine[layer])
        h, route = _mix_ln_route(a, w_out.astype(BF16), h, gam[0:1], bet[0:1], w_route)
        h, h_bf = _moe_layer(h, route, moe_w_gate.astype(F32), moe_w_up.astype(F32), moe_w_down.astype(F32), layer,
                             gam[1:2], bet[1:2])
    return h.reshape(b_, s_, d_)
```

```python
import functools
import math

import jax
import jax.numpy as jnp
from jax import lax
from jax.experimental import pallas as pl
from jax.experimental.pallas import tpu as pltpu

F32 = jnp.float32
BF16 = jnp.bfloat16
HIGHEST = lax.Precision.HIGHEST

D_MODEL = 1024
DEPTH = 4
N_MIXERS = 3
DEEPNORM_ALPHA = (2 * DEPTH) ** 0.25
NORM_EPS = 1e-5
LOG2_E = 1.4426950408889634

ATT_HEAD_DIM = 64
ATT_HEADS = D_MODEL // ATT_HEAD_DIM
DILATION_CONFIGS = ((128, 1), (512, 4), (2048, 16))
N_DIL_GROUPS = len(DILATION_CONFIGS)
ATT_BLOCK = 128
ATT_STEP_BLOCKS = 2
REL_BUCKETS = 32
REL_MAX_DIST = 2048

SSM_D_INNER = 2 * D_MODEL
SSM_HEAD_DIM = 64
SSM_HEADS = SSM_D_INNER // SSM_HEAD_DIM
SSM_GROUPS = 8
SSM_HEADS_PER_GROUP = SSM_HEADS // SSM_GROUPS
SSM_STATE = 128
SSM_CONV = 4
SSM_CHUNK = 128
SSM_BC_DIM = 2 * SSM_GROUPS * SSM_STATE

HGRN_HEAD_DIM = 128
HGRN_HEADS = D_MODEL // HGRN_HEAD_DIM
GLA_CHUNK = 128
GLA_SUB = 8

MOE_GROUPS = 8
MOE_EXPERTS_PER_GROUP = 8
MOE_EXPERTS = MOE_GROUPS * MOE_EXPERTS_PER_GROUP
MOE_TOP_K = 2
MOE_D_FF = 512
MOE_BLOCK = 256

LANES = 128
SUBLANES = 8
VMEM_LIMIT_BYTES = 52 * 1024 * 1024


def _cparams(*sem):
    return pltpu.CompilerParams(dimension_semantics=sem, vmem_limit_bytes=VMEM_LIMIT_BYTES)


def _silu(x):
    hx = 0.5 * x
    return hx + hx * jnp.tanh(hx)


def _split_bf16(v, parts):
    out = []
    for _ in range(parts - 1):
        p = v.astype(BF16)
        out.append(p)
        v = v - p.astype(F32)
    out.append(v.astype(BF16))
    return out


def _dot_01_left(sel3, v):
    return jnp.dot(sel3, jnp.concatenate(_split_bf16(v, 3), axis=0), preferred_element_type=F32)


def _dot_01_right(v, sel3):
    return jnp.dot(jnp.concatenate(_split_bf16(v, 3), axis=1), sel3, preferred_element_type=F32)


def _floor_to_pow2(v, m):
    assert m & (m - 1) == 0
    return jnp.bitwise_and(v, -m)


def _mod_pow2(v, m):
    assert m & (m - 1) == 0
    return jnp.bitwise_and(v, m - 1)


def _mm_kernel(x_ref, w_ref, o_ref):
    o_ref[...] = jnp.dot(x_ref[...], w_ref[...], preferred_element_type=F32).astype(o_ref.dtype)


def _matmul(x, w, out_dtype, tm=1024, tn=1024):
    m, k = x.shape
    n = w.shape[1]
    tm = min(tm, m)
    tn = min(tn, n)
    assert m % tm == 0 and n % tn == 0
    return pl.pallas_call(
        _mm_kernel,
        grid=(n // tn, m // tm),
        in_specs=[pl.BlockSpec((tm, k), lambda j, i: (i, 0)),
                  pl.BlockSpec((k, tn), lambda j, i: (0, j))],
        out_specs=pl.BlockSpec((tm, tn), lambda j, i: (i, j)),
        out_shape=jax.ShapeDtypeStruct((m, n), out_dtype),
        compiler_params=_cparams("arbitrary", "arbitrary"),
        name="mm",
    )(x, w)


def _layer_norm(y, gamma, beta):
    mu = jnp.mean(y, axis=-1, keepdims=True)
    yc = y - mu
    var = jnp.mean(yc * yc, axis=-1, keepdims=True)
    return yc * lax.rsqrt(var + NORM_EPS) * gamma + beta


def _mix_ln_route_kernel(a_ref, w_ref, h_ref, gam_ref, bet_ref, wr_ref, hout_ref, route_ref):
    mix = jnp.dot(a_ref[...], w_ref[...], preferred_element_type=F32)
    hn = _layer_norm(DEEPNORM_ALPHA * h_ref[...] + mix, gam_ref[...], bet_ref[...])
    hout_ref[...] = hn
    hn_hi, hn_lo = _split_bf16(hn, 2)
    both = jnp.dot(hn_hi, wr_ref[...], preferred_element_type=F32)
    logits = (both[:, :LANES] + both[:, LANES:]
              + jnp.dot(hn_lo, wr_ref[:, :LANES], preferred_element_type=F32))
    lane = lax.broadcasted_iota(jnp.int32, logits.shape, 1).astype(F32)
    neg = -jnp.inf
    l1 = jnp.where(lane < MOE_GROUPS, logits, neg)
    m1 = jnp.max(l1, axis=-1, keepdims=True)
    grp = jnp.min(jnp.where(l1 == m1, lane, float(LANES)), axis=-1, keepdims=True)
    g1 = 1.0 / jnp.sum(jnp.exp(l1 - m1), axis=-1, keepdims=True)
    lo = MOE_GROUPS + grp * MOE_EXPERTS_PER_GROUP
    l2 = jnp.where(lane >= lo, jnp.where(lane < lo + MOE_EXPERTS_PER_GROUP, logits, neg), neg)
    t1 = jnp.max(l2, axis=-1, keepdims=True)
    i1 = jnp.min(jnp.where(l2 == t1, lane, float(LANES)), axis=-1, keepdims=True)
    l2b = jnp.where(lane == i1, neg, l2)
    t2 = jnp.max(l2b, axis=-1, keepdims=True)
    i2 = jnp.min(jnp.where(l2b == t2, lane, float(LANES)), axis=-1, keepdims=True)
    e21 = jnp.exp(t2 - t1)
    ga = g1 / (1.0 + e21)
    gb = g1 * e21 / (1.0 + e21)
    out = jnp.where(lane == 0, i1 - MOE_GROUPS,
                    jnp.where(lane == 1, i2 - MOE_GROUPS,
                              jnp.where(lane == 2, ga, jnp.where(lane == 3, gb, 0.0))))
    route_ref[...] = out


def _router_weights(w_coarse, w_fine):
    w = jnp.pad(jnp.concatenate([w_coarse, w_fine], axis=1).astype(F32),
                ((0, 0), (0, LANES - MOE_GROUPS - MOE_EXPERTS)))
    hi = w.astype(BF16)
    lo = (w - hi.astype(F32)).astype(BF16)
    return jnp.concatenate([hi, lo], axis=1)


def _mix_ln_route(a, w, h, gamma, beta, w_route, tm=512):
    s, k = a.shape
    d = w.shape[1]
    return pl.pallas_call(
        _mix_ln_route_kernel,
        grid=(s // tm,),
        in_specs=[pl.BlockSpec((tm, k), lambda i: (i, 0)),
                  pl.BlockSpec((k, d), lambda i: (0, 0)),
                  pl.BlockSpec((tm, d), lambda i: (i, 0)),
                  pl.BlockSpec((1, d), lambda i: (0, 0)),
                  pl.BlockSpec((1, d), lambda i: (0, 0)),
                  pl.BlockSpec((d, 2 * LANES), lambda i: (0, 0))],
        out_specs=[pl.BlockSpec((tm, d), lambda i: (i, 0)),
                   pl.BlockSpec((tm, LANES), lambda i: (i, 0))],
        out_shape=[jax.ShapeDtypeStruct((s, d), F32),
                   jax.ShapeDtypeStruct((s, LANES), F32)],
        compiler_params=_cparams("arbitrary"),
        name="mix_ln_route",
    )(a, w, h, gamma, beta, w_route)


def _start_row_gather(src_hbm, row_of, dst_vmem, sem):
    for i in range(dst_vmem.shape[0]):
        pltpu.make_async_copy(src_hbm.at[pl.ds(row_of(i), 1), :], dst_vmem.at[pl.ds(i, 1), :], sem).start(
            priority=i % 2)


def _wait_row_gather(dst_vmem, sem):
    pltpu.make_async_copy(dst_vmem, dst_vmem, sem).wait()


PAD_ROW_STRIDE = 61
MOE_GATHER_BUFFERS = 4
MOE_WEIGHT_SLOTS = 3


def _moe_kernel(bpos_ref, eseq_ref, tok_ref, cnt_ref, h_hbm, wg_hbm, wu_hbm, wd_hbm, y_ref,
                xbuf0, xbuf1, xbuf2, xbuf3, wg_f, wu_f, wd_f, wg_bf, wu_bf, wd_bf, gsems, wsems, *, layer):
    b = pl.program_id(0)
    n_used = cnt_ref[0]
    n_exp = cnt_ref[1]
    bufs = (xbuf0, xbuf1, xbuf2, xbuf3)
    depth = MOE_GATHER_BUFFERS - 1

    def start_gather(blk, r):
        _start_row_gather(h_hbm, lambda i: tok_ref[blk * MOE_BLOCK + i], bufs[r], gsems.at[r])

    def weight_copies(p):
        e = eseq_ref[p]
        slot = lax.rem(p, MOE_WEIGHT_SLOTS)
        return [pltpu.make_async_copy(w.at[layer, e], f.at[slot], wsems.at[slot])
                for w, f in ((wg_hbm, wg_f), (wu_hbm, wu_f), (wd_hbm, wd_f))]

    @pl.when(b == 0)
    def _():
        for r in range(depth):
            start_gather(jnp.minimum(r, n_used - 1), r)
        for c in weight_copies(0):
            c.start()

    @pl.when(jnp.logical_and(b == 0, n_exp > 1))
    def _():
        for c in weight_copies(1):
            c.start()

    p = bpos_ref[b]
    changed = jnp.logical_or(b == 0, p != bpos_ref[jnp.maximum(b - 1, 0)])

    @pl.when(jnp.logical_and(changed, p + 2 < n_exp))
    def _():
        for c in weight_copies(p + 2):
            c.start()

    @pl.when(changed)
    def _():
        for c in weight_copies(p):
            c.wait()
        slot = lax.rem(p, MOE_WEIGHT_SLOTS)
        wg_bf[...] = wg_f[slot].astype(BF16)
        wu_bf[...] = wu_f[slot].astype(BF16)
        wd_bf[...] = wd_f[slot].astype(BF16)

    for r in range(MOE_GATHER_BUFFERS):
        @pl.when(jnp.logical_and(b < n_used, lax.rem(b, MOE_GATHER_BUFFERS) == r))
        def _():
            _wait_row_gather(bufs[r], gsems.at[r])
            start_gather(jnp.minimum(b + depth, n_used - 1), (r + depth) % MOE_GATHER_BUFFERS)
            x = bufs[r][...].astype(BF16)
            g = jnp.dot(x, wg_bf[...], preferred_element_type=F32)
            u = jnp.dot(x, wu_bf[...], preferred_element_type=F32)
            hid = (_silu(g) * u).astype(BF16)
            y_ref[...] = jnp.dot(hid, wd_bf[...], preferred_element_type=F32)

        @pl.when(jnp.logical_and(b == n_used - 1, lax.rem(b, MOE_GATHER_BUFFERS) == r))
        def _():
            for ahead in range(1, MOE_GATHER_BUFFERS):
                rr = (r + ahead) % MOE_GATHER_BUFFERS
                _wait_row_gather(bufs[rr], gsems.at[rr])

    @pl.when(b >= n_used)
    def _():
        y_ref[...] = jnp.zeros_like(y_ref)


def _moe_experts(h, block_pos, expert_seq, slot_token, counts, w_gate, w_up, w_down, layer):
    s, d = h.shape
    n_slots = slot_token.shape[0]
    n_blocks = n_slots // MOE_BLOCK
    dff = w_gate.shape[3]
    hbm = pl.BlockSpec(memory_space=pl.ANY)
    grid_spec = pltpu.PrefetchScalarGridSpec(
        num_scalar_prefetch=4,
        grid=(n_blocks,),
        in_specs=[hbm, hbm, hbm, hbm],
        out_specs=pl.BlockSpec((MOE_BLOCK, d), lambda b, *_: (b, 0)),
        scratch_shapes=([pltpu.VMEM((MOE_BLOCK, d), F32)] * MOE_GATHER_BUFFERS
                        + [pltpu.VMEM((MOE_WEIGHT_SLOTS, d, dff), F32), pltpu.VMEM((MOE_WEIGHT_SLOTS, d, dff), F32),
                           pltpu.VMEM((MOE_WEIGHT_SLOTS, dff, d), F32),
                           pltpu.VMEM((d, dff), BF16), pltpu.VMEM((d, dff), BF16), pltpu.VMEM((dff, d), BF16),
                           pltpu.SemaphoreType.DMA((MOE_GATHER_BUFFERS,)),
                           pltpu.SemaphoreType.DMA((MOE_WEIGHT_SLOTS,))]),
    )
    return pl.pallas_call(
        functools.partial(_moe_kernel, layer=layer),
        grid_spec=grid_spec,
        out_shape=jax.ShapeDtypeStruct((n_slots, d), F32),
        compiler_params=_cparams("arbitrary"),
        name="moe_experts",
    )(block_pos, expert_seq, slot_token, counts, h, w_gate, w_up, w_down)


COMBINE_TILE = 256


COMBINE_BUFFERS = 3


def _combine_ln_kernel(pos_ref, y_hbm, h_ref, route_ref, gam_ref, bet_ref, hout_ref, hbf_ref, *scratch):
    t = pl.program_id(0)
    n_t = pl.num_programs(0)
    sems = scratch[-1]
    bufs = [scratch[r * MOE_TOP_K:(r + 1) * MOE_TOP_K] for r in range(COMBINE_BUFFERS)]
    depth = COMBINE_BUFFERS - 1

    def start_gather(tile, r):
        for k in range(MOE_TOP_K):
            _start_row_gather(y_hbm, lambda i: pos_ref[(tile * COMBINE_TILE + i) * MOE_TOP_K + k],
                              bufs[r][k], sems.at[r])

    def wait_gather(r):
        for k in range(MOE_TOP_K):
            _wait_row_gather(bufs[r][k], sems.at[r])

    @pl.when(t == 0)
    def _():
        for r in range(depth):
            start_gather(jnp.minimum(r, n_t - 1), r)

    for r in range(COMBINE_BUFFERS):
        @pl.when(lax.rem(t, COMBINE_BUFFERS) == r)
        def _():
            wait_gather(r)
            start_gather(jnp.minimum(t + depth, n_t - 1), (r + depth) % COMBINE_BUFFERS)
            route = route_ref[...]
            ffn = route[:, 2:3] * bufs[r][0][...] + route[:, 3:4] * bufs[r][1][...]
            hn = _layer_norm(DEEPNORM_ALPHA * h_ref[...] + ffn, gam_ref[...], bet_ref[...])
            hout_ref[...] = hn
            hbf_ref[...] = hn.astype(BF16)

        @pl.when(jnp.logical_and(t == n_t - 1, lax.rem(t, COMBINE_BUFFERS) == r))
        def _():
            for ahead in range(1, COMBINE_BUFFERS):
                wait_gather((r + ahead) % COMBINE_BUFFERS)


def _combine_ln(y_slots, pos, h, route, gamma, beta):
    s, d = h.shape
    tm = COMBINE_TILE
    grid_spec = pltpu.PrefetchScalarGridSpec(
        num_scalar_prefetch=1,
        grid=(s // tm,),
        in_specs=[pl.BlockSpec(memory_space=pl.ANY),
                  pl.BlockSpec((tm, d), lambda i, pos: (i, 0)),
                  pl.BlockSpec((tm, LANES), lambda i, pos: (i, 0)),
                  pl.BlockSpec((1, d), lambda i, pos: (0, 0)),
                  pl.BlockSpec((1, d), lambda i, pos: (0, 0))],
        out_specs=[pl.BlockSpec((tm, d), lambda i, pos: (i, 0)),
                   pl.BlockSpec((tm, d), lambda i, pos: (i, 0))],
        scratch_shapes=([pltpu.VMEM((tm, d), F32)] * (COMBINE_BUFFERS * MOE_TOP_K)
                        + [pltpu.SemaphoreType.DMA((COMBINE_BUFFERS,))]),
    )
    return pl.pallas_call(
        _combine_ln_kernel,
        grid_spec=grid_spec,
        out_shape=[jax.ShapeDtypeStruct((s, d), F32), jax.ShapeDtypeStruct((s, d), BF16)],
        compiler_params=_cparams("arbitrary"),
        name="combine_ln",
    )(pos, y_slots, h, route, gamma, beta)


def _dispatch_tables(route):
    s = route.shape[0]
    n_asg = s * MOE_TOP_K
    expert = route[:, :MOE_TOP_K].astype(jnp.int32).reshape(n_asg)
    onehot = (expert[:, None] == jnp.arange(MOE_EXPERTS, dtype=jnp.int32)[None, :]).astype(jnp.int32)
    csum = jnp.cumsum(onehot, axis=0)
    counts = csum[-1]
    padded = (counts + MOE_BLOCK - 1) // MOE_BLOCK * MOE_BLOCK
    pend = jnp.cumsum(padded)
    pstart = pend - padded
    pos = jnp.sum(onehot * (csum - 1 + pstart[None, :]), axis=1).astype(jnp.int32)
    n_blocks = (n_asg + MOE_EXPERTS * (MOE_BLOCK - 1) + MOE_BLOCK - 1) // MOE_BLOCK
    n_slots = n_blocks * MOE_BLOCK
    token_ids = jnp.arange(n_asg, dtype=jnp.int32) // MOE_TOP_K
    filler = (jnp.arange(n_slots, dtype=jnp.int32) * PAD_ROW_STRIDE) % s
    slot_token = filler.at[pos].set(token_ids, unique_indices=True)
    block_start = jnp.arange(n_blocks, dtype=jnp.int32) * MOE_BLOCK
    block_expert = jnp.sum((pend[None, :] <= block_start[:, None]).astype(jnp.int32), axis=1)
    block_expert = jnp.minimum(block_expert, MOE_EXPERTS - 1).astype(jnp.int32)
    present = counts > 0
    seq_pos = jnp.cumsum(present.astype(jnp.int32)) - 1
    n_exp = seq_pos[-1] + 1
    expert_ids = jnp.arange(MOE_EXPERTS, dtype=jnp.int32)
    expert_seq = jnp.zeros((MOE_EXPERTS,), jnp.int32).at[jnp.where(present, seq_pos, MOE_EXPERTS)].set(
        expert_ids, mode='drop')
    block_pos = jnp.sum(jnp.where(block_expert[:, None] == expert_ids[None, :], seq_pos[None, :], 0), axis=1)
    block_pos = jnp.minimum(block_pos, n_exp - 1).astype(jnp.int32)
    counts2 = jnp.stack([pend[-1] // MOE_BLOCK, n_exp]).astype(jnp.int32)
    return pos, slot_token, block_pos, expert_seq, counts2


def _moe_layer(h, route, w_gate, w_up, w_down, layer, gamma, beta):
    pos, slot_token, block_pos, expert_seq, counts = _dispatch_tables(route)
    y_slots = _moe_experts(h, block_pos, expert_seq, slot_token, counts, w_gate, w_up, w_down, layer)
    return _combine_ln(y_slots, pos, h, route, gamma, beta)


def _t5_bucket(dist):
    n = jnp.maximum(dist, 0)
    max_exact = REL_BUCKETS // 2
    ratio = jnp.maximum(n, max_exact).astype(F32) / max_exact
    large = max_exact + (jnp.log(ratio) / math.log(REL_MAX_DIST / max_exact) * (REL_BUCKETS - max_exact)).astype(jnp.int32)
    large = jnp.minimum(large, REL_BUCKETS - 1)
    return jnp.where(n < max_exact, n, large)


def _bias_kernel(idx_ref, tab_ref, o_ref):
    idx = idx_ref[0]
    tab = tab_ref[0]
    acc = jnp.zeros(o_ref.shape[1:], F32)
    for b in range(REL_BUCKETS):
        acc = jnp.where(idx == b, tab[:, b:b + 1], acc)
    p = lax.broadcasted_iota(jnp.int32, acc.shape, 1)
    ik = _mod_pow2(p, 2 * ATT_BLOCK)
    iq = lax.shift_right_logical(p, int(math.log2(2 * ATT_BLOCK)))
    rel = iq + ATT_BLOCK - ik
    o_ref[0] = jnp.where((rel >= 0) & (rel <= ATT_BLOCK), acc, -jnp.inf)


def _attention_bias(rel_bias):
    iq = jnp.arange(ATT_BLOCK)[:, None]
    ik = jnp.arange(2 * ATT_BLOCK)[None, :]
    rel = iq + ATT_BLOCK - ik
    idx = jnp.stack([_t5_bucket(rel * dil) for _, dil in DILATION_CONFIGS], 0)
    n_pairs = ATT_BLOCK * 2 * ATT_BLOCK
    idx = idx.reshape(N_DIL_GROUPS, 1, n_pairs).astype(jnp.int32)
    tab = jnp.transpose(rel_bias.astype(F32), (1, 2, 0))
    bias = pl.pallas_call(
        _bias_kernel,
        grid=(N_DIL_GROUPS,),
        in_specs=[pl.BlockSpec((1, 1, n_pairs), lambda g: (g, 0, 0)),
                  pl.BlockSpec((1, ATT_HEADS, REL_BUCKETS), lambda g: (g, 0, 0))],
        out_specs=pl.BlockSpec((1, ATT_HEADS, n_pairs), lambda g: (g, 0, 0)),
        out_shape=jax.ShapeDtypeStruct((N_DIL_GROUPS, ATT_HEADS, n_pairs), F32),
        compiler_params=_cparams("arbitrary"),
        name="attn_bias",
    )(idx, tab)
    return bias.reshape(N_DIL_GROUPS, ATT_HEADS, ATT_BLOCK, 2 * ATT_BLOCK)


LSE_LANES_PER_HEAD = LANES // ATT_HEADS


def _lse_lane(h):
    return (h & 1) * ATT_HEAD_DIM + (h >> 1) * LSE_LANES_PER_HEAD


def _attn_kernel(q_ref, kc_ref, vc_ref, bias_ref, o_ref, lse_ref, kprev, vprev):
    n = pl.program_id(1)
    pair_w = 2 * ATT_HEAD_DIM
    assert pair_w == LANES and 2 * LSE_LANES_PER_HEAD * (ATT_HEADS // 2) == LANES
    dn = (((1,), (1,)), ((), ()))

    def attend(rows, prev):
        n_keys = (2 if prev else 1) * ATT_BLOCK
        lane_k = lax.broadcasted_iota(jnp.int32, (n_keys, pair_w), 1)
        head_a = lane_k < ATT_HEAD_DIM
        ones_a = jnp.where(head_a, 1.0, 0.0).astype(BF16)
        ones_b = jnp.where(head_a, 0.0, 1.0).astype(BF16)
        lane_q = lax.broadcasted_iota(jnp.int32, (ATT_BLOCK, pair_w), 1)
        zero = jnp.zeros((n_keys, pair_w), BF16)
        lse_parts = []
        for p in range(ATT_HEADS // 2):
            ps = slice(p * pair_w, (p + 1) * pair_w)
            q = q_ref[rows, ps] * (ATT_HEAD_DIM ** -0.5)
            k = kc_ref[rows, ps]
            v = vc_ref[rows, ps]
            if prev:
                k = jnp.concatenate([prev[0][prev[2], ps], k], axis=0)
                v = jnp.concatenate([prev[1][prev[2], ps], v], axis=0)
            k_ab = jnp.concatenate([jnp.where(head_a, k, zero), jnp.where(head_a, zero, k)], axis=0)
            s = lax.dot_general(q, k_ab, dn, preferred_element_type=F32)
            key0 = 0 if prev else ATT_BLOCK
            s_a = s[:, :n_keys] + bias_ref[0, 2 * p][:, key0:]
            s_b = s[:, n_keys:] + bias_ref[0, 2 * p + 1][:, key0:]
            if prev:
                m_a = jnp.max(jnp.maximum(s_a[:, :ATT_BLOCK], s_a[:, ATT_BLOCK:]), axis=-1, keepdims=True)
                m_b = jnp.max(jnp.maximum(s_b[:, :ATT_BLOCK], s_b[:, ATT_BLOCK:]), axis=-1, keepdims=True)
            else:
                m_a = jnp.max(s_a, axis=-1, keepdims=True)
                m_b = jnp.max(s_b, axis=-1, keepdims=True)
            p_a = jnp.exp(s_a - m_a).astype(BF16)
            p_b = jnp.exp(s_b - m_b).astype(BF16)
            rhs_a = jnp.concatenate([jnp.where(head_a, v, zero), ones_a], axis=1)
            rhs_b = jnp.concatenate([jnp.where(head_a, zero, v), ones_b], axis=1)
            acc = (jnp.dot(p_a, rhs_a, preferred_element_type=F32)
                   + jnp.dot(p_b, rhs_b, preferred_element_type=F32))
            l = acc[:, pair_w:]
            o_ref[rows, ps] = (acc[:, :pair_w] / l).astype(o_ref.dtype)
            lse_pair = jnp.where(lane_q < ATT_HEAD_DIM, m_a, m_b) + jnp.log(l)
            in_pair = _floor_to_pow2(_mod_pow2(lane_q, ATT_HEAD_DIM), LSE_LANES_PER_HEAD) == p * LSE_LANES_PER_HEAD
            lse_parts.append(jnp.where(in_pair, lse_pair, 0.0))
        while len(lse_parts) > 1:
            lse_parts = [a + b for a, b in zip(lse_parts[::2], lse_parts[1::2])]
        lse_ref[rows, :] = lse_parts[0]

    def run(first_has_prev):
        for sub in range(ATT_STEP_BLOCKS):
            rows = slice(sub * ATT_BLOCK, (sub + 1) * ATT_BLOCK)
            if sub > 0:
                prev = (kc_ref, vc_ref, slice((sub - 1) * ATT_BLOCK, sub * ATT_BLOCK))
            else:
                prev = (kprev, vprev, slice(None)) if first_has_prev else None
            attend(rows, prev)
        last = slice((ATT_STEP_BLOCKS - 1) * ATT_BLOCK, ATT_STEP_BLOCKS * ATT_BLOCK)
        kprev[...] = kc_ref[last, :]
        vprev[...] = vc_ref[last, :]

    @pl.when(n > 0)
    def _():
        run(True)

    @pl.when(n == 0)
    def _():
        run(False)


def _dilated_branch(qkv, bias, g, dilation):
    s = qkv.shape[0]
    hd = ATT_HEADS * ATT_HEAD_DIM
    sub_len = s // dilation
    step_rows = ATT_STEP_BLOCKS * ATT_BLOCK
    n_steps = sub_len // step_rows
    assert s % dilation == 0 and sub_len % step_rows == 0

    def cur(j):
        return pl.BlockSpec((step_rows, hd), lambda r, n: (r * n_steps + n, j))

    return pl.pallas_call(
        _attn_kernel,
        grid=(dilation, n_steps),
        in_specs=[cur(0), cur(1), cur(2),
                  pl.BlockSpec((1, ATT_HEADS, ATT_BLOCK, 2 * ATT_BLOCK), lambda r, n: (g, 0, 0, 0))],
        out_specs=[pl.BlockSpec((step_rows, hd), lambda r, n: (r * n_steps + n, 0)),
                   pl.BlockSpec((step_rows, LANES), lambda r, n: (r * n_steps + n, 0))],
        out_shape=[jax.ShapeDtypeStruct((s, hd), BF16), jax.ShapeDtypeStruct((s, LANES), F32)],
        scratch_shapes=[pltpu.VMEM((ATT_BLOCK, hd), BF16), pltpu.VMEM((ATT_BLOCK, hd), BF16)],
        compiler_params=_cparams("arbitrary", "arbitrary"),
        name=f"dilated_attn_g{g}",
    )(qkv, qkv, qkv, bias)


def _attn_combine_kernel(o0, o1, o2, l0, l1, l2, out_ref):
    a0, a1, a2 = l0[...], l1[...], l2[...]
    m = jnp.maximum(jnp.maximum(a0, a1), a2)
    e0, e1, e2 = jnp.exp(a0 - m), jnp.exp(a1 - m), jnp.exp(a2 - m)
    inv = 1.0 / (e0 + e1 + e2)
    hd = out_ref.shape[1]
    r = lax.broadcasted_iota(jnp.int32, (LANES, hd), 0)
    c = lax.broadcasted_iota(jnp.int32, (LANES, hd), 1)
    head = lax.shift_right_logical(c, int(math.log2(ATT_HEAD_DIM)))
    widen = jnp.where(r == _lse_lane(head), 1.0, 0.0)
    widen3 = jnp.concatenate([widen.astype(BF16)] * 3, axis=0)
    acc = jnp.zeros(out_ref.shape, F32)
    for e, o in ((e0, o0), (e1, o1), (e2, o2)):
        acc = acc + _dot_01_right(e * inv, widen3) * o[...].astype(F32)
    out_ref[...] = acc.astype(out_ref.dtype)


def _attn_combine(outs, lses, tm=512):
    s, d = outs[0].shape
    spec = pl.BlockSpec((tm, d), lambda i: (i, 0))
    lspec = pl.BlockSpec((tm, LANES), lambda i: (i, 0))
    return pl.pallas_call(
        _attn_combine_kernel,
        grid=(s // tm,),
        in_specs=[spec] * 3 + [lspec] * 3,
        out_specs=spec,
        out_shape=jax.ShapeDtypeStruct((s, d), BF16),
        compiler_params=_cparams("arbitrary"),
        name="attn_combine",
    )(*outs, *lses)


def _to_strided_order(x, dilation):
    s, c = x.shape
    return x.reshape(s // dilation, dilation, c).transpose(1, 0, 2).reshape(s, c)


def _from_strided_order(x, dilation):
    s, c = x.shape
    return x.reshape(dilation, s // dilation, c).transpose(1, 0, 2).reshape(s, c)


def _dilated_attention(h_bf, w_in_bf, bias):
    per_group = 3 * ATT_HEADS * ATT_HEAD_DIM
    outs, lses = [], []
    for g, (_, dilation) in enumerate(DILATION_CONFIGS):
        qkv = _matmul(_to_strided_order(h_bf, dilation), w_in_bf[:, g * per_group:(g + 1) * per_group], BF16)
        o, l = _dilated_branch(qkv, bias, g, dilation)
        outs.append(_from_strided_order(o, dilation))
        lses.append(_from_strided_order(l, dilation))
    return _attn_combine(outs, lses)


def _head_expand(n_heads, width):
    r = lax.broadcasted_iota(jnp.int32, (LANES, n_heads * width), 0)
    c = lax.broadcasted_iota(jnp.int32, (LANES, n_heads * width), 1)
    return jnp.where(_floor_to_pow2(c, width) == r * width, 1.0, 0.0).astype(BF16)


def _causal_conv_silu(cur, halo, w, b):
    rows = lax.broadcasted_iota(jnp.int32, (SUBLANES, cur.shape[1]), 0)
    acc = cur * w[SSM_CONV - 1:SSM_CONV, :] + b
    for j in range(1, SSM_CONV):
        shifted = pltpu.roll(cur, j, 0)
        head = jnp.where(rows < j, pltpu.roll(halo, j, 0), shifted[:SUBLANES])
        shifted = jnp.concatenate([head, shifted[SUBLANES:]], axis=0)
        acc = acc + shifted * w[SSM_CONV - 1 - j:SSM_CONV - j, :]
    return _silu(acc)


def _ssd_kernel(z_ref, x_ref, bc_ref, xh_ref, bch_ref, dt_ref, cwx_ref, cbx_ref, cwbc_ref, cbbc_ref,
                dtb_ref, aneg_ref, dskip_ref, nw_ref, o_ref, state_ref):
    c = pl.program_id(0)

    @pl.when(c == 0)
    def _():
        state_ref[...] = jnp.zeros_like(state_ref)

    first = c == 0
    xh = jnp.where(first, 0.0, xh_ref[...].astype(F32))
    bch = jnp.where(first, 0.0, bch_ref[...].astype(F32))
    xs = _causal_conv_silu(x_ref[...].astype(F32), xh, cwx_ref[...], cbx_ref[...])
    bc = _causal_conv_silu(bc_ref[...].astype(F32), bch, cwbc_ref[...], cbbc_ref[...])
    gn = SSM_GROUPS * SSM_STATE

    x_dt = dt_ref[...] + dtb_ref[...]
    dt = jnp.maximum(x_dt, 0.0) + jnp.log1p(jnp.exp(-jnp.abs(x_dt)))
    a = dt * aneg_ref[...]
    row = lax.broadcasted_iota(jnp.int32, (SSM_CHUNK, SSM_CHUNK), 0)
    colv = lax.broadcasted_iota(jnp.int32, (SSM_CHUNK, SSM_CHUNK), 1)
    causal = colv <= row
    tri = jnp.where(causal, 1.0, 0.0).astype(BF16)
    a_cs = _dot_01_left(jnp.concatenate([tri] * 3, axis=1), a)
    a_cs_t = a_cs.T
    expand = _head_expand(SSM_HEADS, SSM_HEAD_DIM)
    a_last = a_cs[SSM_CHUNK - 1:SSM_CHUNK, :]
    per_head = jnp.concatenate([dt, jnp.exp(a_cs), jnp.exp(a_last - a_cs)], axis=0)
    per_head_e = _dot_01_right(per_head, jnp.concatenate([expand] * 3, axis=0))
    dt_e = per_head_e[:SSM_CHUNK]
    in_decay_e = per_head_e[SSM_CHUNK:2 * SSM_CHUNK]
    out_decay_e = per_head_e[2 * SSM_CHUNK:]
    chunk_decay_e = in_decay_e[SSM_CHUNK - 1:SSM_CHUNK, :]

    xdt = xs * dt_e
    xdt_bf = xdt.astype(BF16)
    xend_bf = (xdt * out_decay_e).astype(BF16)
    gw = SSM_HEADS_PER_GROUP * SSM_HEAD_DIM
    ys = []
    for g in range(SSM_GROUPS):
        b_g = bc[:, g * SSM_STATE:(g + 1) * SSM_STATE]
        c_g = bc[:, gn + g * SSM_STATE:gn + (g + 1) * SSM_STATE].astype(BF16)
        cb = lax.dot_general(c_g, b_g.astype(BF16), (((1,), (1,)), ((), ())), preferred_element_type=F32)
        prev = state_ref[g]
        y_off = jnp.dot(c_g, prev.astype(BF16), preferred_element_type=F32) * in_decay_e[:, g * gw:(g + 1) * gw]
        y_parts = []
        for r in range(SSM_HEADS_PER_GROUP):
            hh = g * SSM_HEADS_PER_GROUP + r
            seg = jnp.exp(jnp.where(causal, a_cs[:, hh:hh + 1] - a_cs_t[hh:hh + 1, :], -jnp.inf))
            y_parts.append(jnp.dot((cb * seg).astype(BF16), xdt_bf[:, hh * SSM_HEAD_DIM:(hh + 1) * SSM_HEAD_DIM],
                                   preferred_element_type=F32))
        ys.append(jnp.concatenate(y_parts, axis=1) + y_off)
        new = jnp.dot(b_g.T.astype(BF16), xend_bf[:, g * gw:(g + 1) * gw], preferred_element_type=F32)
        state_ref[g] = prev * chunk_decay_e[:, g * gw:(g + 1) * gw] + new
    y = jnp.concatenate(ys, axis=1) + xs * dskip_ref[...]
    u = y * _silu(z_ref[...].astype(F32))
    gsz = SSM_D_INNER // SSM_GROUPS
    outs = []
    for g in range(SSM_GROUPS):
        ug = u[:, g * gsz:(g + 1) * gsz]
        outs.append(ug * lax.rsqrt(jnp.mean(ug * ug, axis=-1, keepdims=True) + NORM_EPS))
    o_ref[...] = (jnp.concatenate(outs, axis=1) * nw_ref[...]).astype(o_ref.dtype)


def _mamba2_inner(zxbc, dt_raw, conv_w, conv_b, dt_bias, a_log, d_skip, norm_w):
    s = zxbc.shape[0]
    n_c = s // SSM_CHUNK
    di = SSM_D_INNER
    halo_blocks = SSM_CHUNK // SUBLANES

    def pad_lanes(v):
        return jnp.pad(v.astype(F32), (0, LANES - v.shape[0])).reshape(1, LANES)

    cw = conv_w.astype(F32)
    cb = conv_b.astype(F32).reshape(1, -1)
    dskip = jnp.repeat(d_skip.astype(F32), SSM_HEAD_DIM).reshape(1, di)
    a_neg = pad_lanes(-jnp.exp(a_log.astype(F32)))

    def halo(colblk):
        return pl.BlockSpec((SUBLANES, di), lambda c: (jnp.maximum(c * halo_blocks - 1, 0), colblk))

    full = lambda shape: pl.BlockSpec(shape, lambda c: (0, 0))
    return pl.pallas_call(
        _ssd_kernel,
        grid=(n_c,),
        in_specs=[pl.BlockSpec((SSM_CHUNK, di), lambda c: (c, 0)),
                  pl.BlockSpec((SSM_CHUNK, di), lambda c: (c, 1)),
                  pl.BlockSpec((SSM_CHUNK, SSM_BC_DIM), lambda c: (c, 2)),
                  halo(1), halo(2),
                  pl.BlockSpec((SSM_CHUNK, LANES), lambda c: (c, 0)),
                  full((SSM_CONV, di)), full((1, di)), full((SSM_CONV, SSM_BC_DIM)), full((1, SSM_BC_DIM)),
                  full((1, LANES)), full((1, LANES)), full((1, di)), full((1, di))],
        out_specs=pl.BlockSpec((SSM_CHUNK, di), lambda c: (c, 0)),
        out_shape=jax.ShapeDtypeStruct((s, di), BF16),
        scratch_shapes=[pltpu.VMEM((SSM_GROUPS, SSM_STATE, SSM_HEADS_PER_GROUP * SSM_HEAD_DIM), F32)],
        compiler_params=_cparams("arbitrary"),
        name="ssd_chunk_scan",
    )(zxbc, zxbc, zxbc, zxbc, zxbc, dt_raw, cw[:, :di], cb[:, :di], cw[:, di:], cb[:, di:],
      pad_lanes(dt_bias), a_neg, dskip, norm_w.astype(F32).reshape(1, di))


def _mamba2_mixer(h_bf, w_in, conv_w, conv_b, dt_bias, a_log, d_skip, norm_w):
    n_main = SSM_D_INNER + SSM_D_INNER + SSM_BC_DIM
    assert SSM_BC_DIM == SSM_D_INNER
    w_main = w_in[:, :n_main].astype(BF16)
    w_dt = jnp.pad(w_in[:, n_main:], ((0, 0), (0, LANES - SSM_HEADS))).astype(BF16)
    zxbc = _matmul(h_bf, w_main, BF16)
    dt_raw = _matmul(h_bf, w_dt, F32)
    return _mamba2_inner(zxbc, dt_raw, conv_w, conv_b, dt_bias, a_log, d_skip, norm_w)


def _gla_kernel(q_ref, f_ref, v_ref, gate_ref, lb_ref, nw_ref, o_ref, state_ref):
    c = pl.program_id(0)

    @pl.when(c == 0)
    def _():
        state_ref[...] = jnp.zeros_like(state_ref)

    L = GLA_CHUNK
    K = HGRN_HEAD_DIM
    n_sub = L // GLA_SUB
    row = lax.broadcasted_iota(jnp.int32, (L, L), 0)
    col = lax.broadcasted_iota(jnp.int32, (L, L), 1)
    tri = jnp.where(col <= row, 1.0, 0.0).astype(BF16)
    tri3 = jnp.concatenate([tri] * 3, axis=1)
    sub3 = lax.broadcasted_iota(jnp.int32, (n_sub, GLA_SUB, L), 1)
    rel3 = (lax.broadcasted_iota(jnp.int32, (n_sub, GLA_SUB, L), 2)
            - lax.broadcasted_iota(jnp.int32, (n_sub, GLA_SUB, L), 0) * GLA_SUB)
    rel3 = jnp.where(rel3 >= 0, jnp.where(rel3 <= sub3, rel3, -1), -1)
    levels = []
    m = 2 * GLA_SUB
    while m <= L:
        levels.append(m)
        m *= 2
    same_block = {m: _floor_to_pow2(row, m) == _floor_to_pow2(col, m) for m in levels if m < L}

    for h in range(HGRN_HEADS):
        hs = slice(h * K, (h + 1) * K)
        f = f_ref[:, hs]
        lb = lb_ref[:, hs]
        q = _silu(q_ref[:, hs].astype(F32))
        v_bf = v_ref[:, hs]
        l1p = jnp.log1p(jnp.exp(-jnp.abs(f)))
        log_k = jnp.log1p(-lb) + jnp.minimum(-f, 0.0) - l1p
        ta = jnp.log(lb)
        tb = jnp.log1p(-lb) + jnp.minimum(f, 0.0) - l1p
        log_f = jnp.maximum(ta, tb) + jnp.log1p(jnp.exp(-jnp.abs(ta - tb)))
        g = _dot_01_left(tri3, log_f * LOG2_E)
        gk = g - log_k * LOG2_E

        g3 = g.reshape(n_sub, GLA_SUB, K)
        q3 = q.reshape(n_sub, GLA_SUB, K)
        gk3 = gk.reshape(n_sub, GLA_SUB, K)
        a3 = jnp.zeros((n_sub, GLA_SUB, L), F32)
        for s in range(GLA_SUB):
            e = jnp.exp2(g3 - gk3[:, s:s + 1, :])
            val = jnp.sum(q3 * e, axis=-1, keepdims=True)
            a3 = jnp.where(rel3 == s, val, a3)
        att = a3.reshape(L, L)

        for m in levels:
            half = m // 2
            zeros = jnp.zeros((half, K), F32)
            q_parts, k_parts = [], []
            for j in range(L // m):
                lo, mid, hi = j * m, j * m + half, (j + 1) * m
                gb = g[mid - 1:mid, :]
                q_parts += [zeros, q[mid:hi] * jnp.exp2(g[mid:hi] - gb)]
                k_parts += [jnp.exp2(gb - gk[lo:mid]), zeros]
            ql = jnp.concatenate(q_parts, axis=0).astype(BF16)
            kl = jnp.concatenate(k_parts, axis=0).astype(BF16)
            al = lax.dot_general(ql, kl, (((1,), (1,)), ((), ())), preferred_element_type=F32)
            att = att + (jnp.where(same_block[m], al, 0.0) if m < L else al)

        state_t = state_ref[h]
        o = (jnp.dot(att.astype(BF16), v_bf, preferred_element_type=F32)
             + lax.dot_general((q * jnp.exp2(g)).astype(BF16), state_t.astype(BF16), (((1,), (1,)), ((), ())),
                               preferred_element_type=F32))
        g_last = g[L - 1:L, :]
        k_dec = jnp.exp2(g_last - gk).astype(BF16)
        v_t = v_bf.astype(F32).T.astype(BF16)
        state_ref[h] = state_t * jnp.exp2(g_last) + jnp.dot(v_t, k_dec, preferred_element_type=F32)
        o = o * lax.rsqrt(jnp.mean(o * o, axis=-1, keepdims=True) + NORM_EPS) * nw_ref[...]
        o_ref[:, hs] = (o * _silu(gate_ref[:, hs].astype(F32))).astype(o_ref.dtype)


def _gla(qig, f, lb, norm_w):
    s, d = f.shape
    n_c = s // GLA_CHUNK
    return pl.pallas_call(
        _gla_kernel,
        grid=(n_c,),
        in_specs=[pl.BlockSpec((GLA_CHUNK, d), lambda c: (c, 0)),
                  pl.BlockSpec((GLA_CHUNK, d), lambda c: (c, 0)),
                  pl.BlockSpec((GLA_CHUNK, d), lambda c: (c, 1)),
                  pl.BlockSpec((GLA_CHUNK, d), lambda c: (c, 2)),
                  pl.BlockSpec((1, d), lambda c: (0, 0)),
                  pl.BlockSpec((1, HGRN_HEAD_DIM), lambda c: (0, 0))],
        out_specs=pl.BlockSpec((GLA_CHUNK, d), lambda c: (c, 0)),
        out_shape=jax.ShapeDtypeStruct((s, d), BF16),
        scratch_shapes=[pltpu.VMEM((HGRN_HEADS, HGRN_HEAD_DIM, HGRN_HEAD_DIM), F32)],
        compiler_params=_cparams("arbitrary"),
        name="gla_chunk_scan",
    )(qig, f, qig, qig, lb, norm_w)


def _hgrn2_mixer(h_bf, w_in, lb, norm_w):
    d = D_MODEL
    w_q, w_f, w_i, w_g = (w_in[:, j * d:(j + 1) * d] for j in range(4))
    qig = _matmul(h_bf, jnp.concatenate([w_q, w_i, w_g], axis=1).astype(BF16), BF16)
    f = _matmul(h_bf, w_f.astype(BF16), F32)
    return _gla(qig, f, lb.astype(F32).reshape(1, d), norm_w.astype(F32).reshape(1, HGRN_HEAD_DIM))


def kernel(x, attn_w_in, attn_w_out, rel_bias, ssm_w_in, ssm_conv_w, ssm_conv_b, ssm_dt_bias, ssm_a_log, ssm_d, ssm_norm_w, ssm_w_out, hgrn_w_in, hgrn_lower_bound, hgrn_norm_w, hgrn_w_out, moe_w_coarse, moe_w_fine, moe_w_gate, moe_w_up, moe_w_down, ln_gamma, ln_beta):
    b_, s_, d_ = x.shape
    assert b_ == 1 and d_ == D_MODEL
    lbs = jax.nn.softmax(hgrn_lower_bound.astype(F32), axis=0)
    lbs = jnp.cumsum(lbs, axis=0) - lbs[0]
    bias = _attention_bias(rel_bias)
    h = x.reshape(s_, d_).astype(F32)
    h_bf = h.astype(BF16)
    for layer in range(DEPTH):
        kind = layer % N_MIXERS
        j = layer // N_MIXERS
        if kind == 0:
            a = _dilated_attention(h_bf, attn_w_in[j].astype(BF16), bias)
            w_out = attn_w_out[j]
        elif kind == 1:
            a = _mamba2_mixer(h_bf, ssm_w_in[j], ssm_conv_w[j], ssm_conv_b[j], ssm_dt_bias[j], ssm_a_log[j],
                              ssm_d[j], ssm_norm_w[j])
            w_out = ssm_w_out[j]
        else:
            a = _hgrn2_mixer(h_bf, hgrn_w_in[j], lbs[layer], hgrn_norm_w[j])
            w_out = hgrn_w_out[j]
        gam = ln_gamma[layer].astype(F32)
        bet = ln_beta[layer].astype(F32)
        w_route = _router_weights(moe_w_coarse[layer], moe_w_fine[layer])
        h, route = _mix_ln_route(a, w_out.astype(BF16), h, gam[0:1], bet[0:1], w_route)
        h, h_bf = _moe_layer(h, route, moe_w_gate.astype(F32), moe_w_up.astype(F32), moe_w_down.astype(F32), layer,
                             gam[1:2], bet[1:2])
    return h.reshape(b_, s_, d_)
```

```python
import functools
import math

import jax
import jax.numpy as jnp
from jax import lax
from jax.experimental import pallas as pl
from jax.experimental.pallas import tpu as pltpu

F32 = jnp.float32
BF16 = jnp.bfloat16
HIGHEST = lax.Precision.HIGHEST

D_MODEL = 1024
DEPTH = 4
N_MIXERS = 3
DEEPNORM_ALPHA = (2 * DEPTH) ** 0.25
NORM_EPS = 1e-5
LOG2_E = 1.4426950408889634

ATT_HEAD_DIM = 64
ATT_HEADS = D_MODEL // ATT_HEAD_DIM
DILATION_CONFIGS = ((128, 1), (512, 4), (2048, 16))
N_DIL_GROUPS = len(DILATION_CONFIGS)
ATT_BLOCK = 128
ATT_STEP_BLOCKS = 4
REL_BUCKETS = 32
REL_MAX_DIST = 2048

SSM_D_INNER = 2 * D_MODEL
SSM_HEAD_DIM = 64
SSM_HEADS = SSM_D_INNER // SSM_HEAD_DIM
SSM_GROUPS = 8
SSM_HEADS_PER_GROUP = SSM_HEADS // SSM_GROUPS
SSM_STATE = 128
SSM_CONV = 4
SSM_CHUNK = 128
SSM_BC_DIM = 2 * SSM_GROUPS * SSM_STATE

HGRN_HEAD_DIM = 128
HGRN_HEADS = D_MODEL // HGRN_HEAD_DIM
GLA_CHUNK = 128
GLA_SUB = 8

MOE_GROUPS = 8
MOE_EXPERTS_PER_GROUP = 8
MOE_EXPERTS = MOE_GROUPS * MOE_EXPERTS_PER_GROUP
MOE_TOP_K = 2
MOE_D_FF = 512
MOE_BLOCK = 256

LANES = 128
SUBLANES = 8
VMEM_LIMIT_BYTES = 52 * 1024 * 1024


def _cparams(*sem):
    return pltpu.CompilerParams(dimension_semantics=sem, vmem_limit_bytes=VMEM_LIMIT_BYTES)


def _silu(x):
    hx = 0.5 * x
    return hx + hx * jnp.tanh(hx)


def _split_bf16(v, parts):
    out = []
    for _ in range(parts - 1):
        p = v.astype(BF16)
        out.append(p)
        v = v - p.astype(F32)
    out.append(v.astype(BF16))
    return out


def _dot_01_left(sel3, v):
    return jnp.dot(sel3, jnp.concatenate(_split_bf16(v, 3), axis=0), preferred_element_type=F32)


def _dot_01_right(v, sel3):
    return jnp.dot(jnp.concatenate(_split_bf16(v, 3), axis=1), sel3, preferred_element_type=F32)


def _floor_to_pow2(v, m):
    assert m & (m - 1) == 0
    return jnp.bitwise_and(v, -m)


def _mod_pow2(v, m):
    assert m & (m - 1) == 0
    return jnp.bitwise_and(v, m - 1)


def _mm_kernel(x_ref, w_ref, o_ref):
    o_ref[...] = jnp.dot(x_ref[...], w_ref[...], preferred_element_type=F32).astype(o_ref.dtype)


def _matmul(x, w, out_dtype, tm=1024, tn=1024):
    m, k = x.shape
    n = w.shape[1]
    tm = min(tm, m)
    tn = min(tn, n)
    assert m % tm == 0 and n % tn == 0
    return pl.pallas_call(
        _mm_kernel,
        grid=(n // tn, m // tm),
        in_specs=[pl.BlockSpec((tm, k), lambda j, i: (i, 0)),
                  pl.BlockSpec((k, tn), lambda j, i: (0, j))],
        out_specs=pl.BlockSpec((tm, tn), lambda j, i: (i, j)),
        out_shape=jax.ShapeDtypeStruct((m, n), out_dtype),
        compiler_params=_cparams("arbitrary", "arbitrary"),
        name="mm",
    )(x, w)


def _layer_norm(y, gamma, beta):
    mu = jnp.mean(y, axis=-1, keepdims=True)
    yc = y - mu
    var = jnp.mean(yc * yc, axis=-1, keepdims=True)
    return yc * lax.rsqrt(var + NORM_EPS) * gamma + beta


def _mix_ln_route_kernel(a_ref, w_ref, h_ref, gam_ref, bet_ref, wr_ref, hout_ref, route_ref):
    mix = jnp.dot(a_ref[...], w_ref[...], preferred_element_type=F32)
    hn = _layer_norm(DEEPNORM_ALPHA * h_ref[...] + mix, gam_ref[...], bet_ref[...])
    hout_ref[...] = hn
    hn_hi, hn_lo = _split_bf16(hn, 2)
    both = jnp.dot(hn_hi, wr_ref[...], preferred_element_type=F32)
    logits = (both[:, :LANES] + both[:, LANES:]
              + jnp.dot(hn_lo, wr_ref[:, :LANES], preferred_element_type=F32))
    lane = lax.broadcasted_iota(jnp.int32, logits.shape, 1).astype(F32)
    neg = -jnp.inf
    l1 = jnp.where(lane < MOE_GROUPS, logits, neg)
    m1 = jnp.max(l1, axis=-1, keepdims=True)
    grp = jnp.min(jnp.where(l1 == m1, lane, float(LANES)), axis=-1, keepdims=True)
    g1 = 1.0 / jnp.sum(jnp.exp(l1 - m1), axis=-1, keepdims=True)
    lo = MOE_GROUPS + grp * MOE_EXPERTS_PER_GROUP
    l2 = jnp.where(lane >= lo, jnp.where(lane < lo + MOE_EXPERTS_PER_GROUP, logits, neg), neg)
    t1 = jnp.max(l2, axis=-1, keepdims=True)
    i1 = jnp.min(jnp.where(l2 == t1, lane, float(LANES)), axis=-1, keepdims=True)
    l2b = jnp.where(lane == i1, neg, l2)
    t2 = jnp.max(l2b, axis=-1, keepdims=True)
    i2 = jnp.min(jnp.where(l2b == t2, lane, float(LANES)), axis=-1, keepdims=True)
    e21 = jnp.exp(t2 - t1)
    ga = g1 / (1.0 + e21)
    gb = g1 * e21 / (1.0 + e21)
    out = jnp.where(lane == 0, i1 - MOE_GROUPS,
                    jnp.where(lane == 1, i2 - MOE_GROUPS,
                              jnp.where(lane == 2, ga, jnp.where(lane == 3, gb, 0.0))))
    route_ref[...] = out


def _router_weights(w_coarse, w_fine):
    w = jnp.pad(jnp.concatenate([w_coarse, w_fine], axis=1).astype(F32),
                ((0, 0), (0, LANES - MOE_GROUPS - MOE_EXPERTS)))
    hi = w.astype(BF16)
    lo = (w - hi.astype(F32)).astype(BF16)
    return jnp.concatenate([hi, lo], axis=1)


def _mix_ln_route(a, w, h, gamma, beta, w_route, tm=512):
    s, k = a.shape
    d = w.shape[1]
    return pl.pallas_call(
        _mix_ln_route_kernel,
        grid=(s // tm,),
        in_specs=[pl.BlockSpec((tm, k), lambda i: (i, 0)),
                  pl.BlockSpec((k, d), lambda i: (0, 0)),
                  pl.BlockSpec((tm, d), lambda i: (i, 0)),
                  pl.BlockSpec((1, d), lambda i: (0, 0)),
                  pl.BlockSpec((1, d), lambda i: (0, 0)),
                  pl.BlockSpec((d, 2 * LANES), lambda i: (0, 0))],
        out_specs=[pl.BlockSpec((tm, d), lambda i: (i, 0)),
                   pl.BlockSpec((tm, LANES), lambda i: (i, 0))],
        out_shape=[jax.ShapeDtypeStruct((s, d), F32),
                   jax.ShapeDtypeStruct((s, LANES), F32)],
        compiler_params=_cparams("arbitrary"),
        name="mix_ln_route",
    )(a, w, h, gamma, beta, w_route)


def _start_row_gather(src_hbm, row_of, dst_vmem, sem):
    for i in range(dst_vmem.shape[0]):
        pltpu.make_async_copy(src_hbm.at[pl.ds(row_of(i), 1), :], dst_vmem.at[pl.ds(i, 1), :], sem).start(
            priority=i % 2)


def _wait_row_gather(dst_vmem, sem):
    pltpu.make_async_copy(dst_vmem, dst_vmem, sem).wait()


PAD_ROW_STRIDE = 61
MOE_GATHER_BUFFERS = 4
MOE_WEIGHT_SLOTS = 3


def _moe_kernel(bpos_ref, brow_ref, eseq_ref, tok_ref, cnt_ref, h_hbm, wg_hbm, wu_hbm, wd_hbm, y_ref,
                xbuf0, xbuf1, xbuf2, xbuf3, wg_f, wu_f, wd_f, wg_bf, wu_bf, wd_bf, gsems, wsems, *, layer):
    b = pl.program_id(0)
    n_used = cnt_ref[0]
    n_exp = cnt_ref[1]
    bufs = (xbuf0, xbuf1, xbuf2, xbuf3)
    depth = MOE_GATHER_BUFFERS - 1

    def start_gather(blk, r):
        row0 = brow_ref[blk]
        _start_row_gather(h_hbm, lambda i: tok_ref[row0 + i], bufs[r], gsems.at[r])

    def weight_copies(p):
        e = eseq_ref[p]
        slot = lax.rem(p, MOE_WEIGHT_SLOTS)
        return [pltpu.make_async_copy(w.at[layer, e], f.at[slot], wsems.at[slot])
                for w, f in ((wg_hbm, wg_f), (wu_hbm, wu_f), (wd_hbm, wd_f))]

    @pl.when(b == 0)
    def _():
        for r in range(depth):
            start_gather(jnp.minimum(r, n_used - 1), r)
        for c in weight_copies(0):
            c.start()

    @pl.when(jnp.logical_and(b == 0, n_exp > 1))
    def _():
        for c in weight_copies(1):
            c.start()

    p = bpos_ref[b]
    changed = jnp.logical_or(b == 0, p != bpos_ref[jnp.maximum(b - 1, 0)])

    @pl.when(jnp.logical_and(changed, p + 2 < n_exp))
    def _():
        for c in weight_copies(p + 2):
            c.start()

    @pl.when(changed)
    def _():
        for c in weight_copies(p):
            c.wait()
        slot = lax.rem(p, MOE_WEIGHT_SLOTS)
        wg_bf[...] = wg_f[slot].astype(BF16)
        wu_bf[...] = wu_f[slot].astype(BF16)
        wd_bf[...] = wd_f[slot].astype(BF16)

    for r in range(MOE_GATHER_BUFFERS):
        @pl.when(jnp.logical_and(b < n_used, lax.rem(b, MOE_GATHER_BUFFERS) == r))
        def _():
            _wait_row_gather(bufs[r], gsems.at[r])
            start_gather(jnp.minimum(b + depth, n_used - 1), (r + depth) % MOE_GATHER_BUFFERS)
            x = bufs[r][...].astype(BF16)
            g = jnp.dot(x, wg_bf[...], preferred_element_type=F32)
            u = jnp.dot(x, wu_bf[...], preferred_element_type=F32)
            hid = (_silu(g) * u).astype(BF16)
            y_ref[...] = jnp.dot(hid, wd_bf[...], preferred_element_type=F32)

        @pl.when(jnp.logical_and(b == n_used - 1, lax.rem(b, MOE_GATHER_BUFFERS) == r))
        def _():
            for ahead in range(1, MOE_GATHER_BUFFERS):
                rr = (r + ahead) % MOE_GATHER_BUFFERS
                _wait_row_gather(bufs[rr], gsems.at[rr])

    @pl.when(b >= n_used)
    def _():
        y_ref[...] = jnp.zeros_like(y_ref)


def _moe_experts(h, block_pos, block_row0, expert_seq, sorted_tokens, counts, w_gate, w_up, w_down, layer):
    s, d = h.shape
    n_blocks = block_pos.shape[0]
    n_slots = n_blocks * MOE_BLOCK
    dff = w_gate.shape[3]
    hbm = pl.BlockSpec(memory_space=pl.ANY)
    grid_spec = pltpu.PrefetchScalarGridSpec(
        num_scalar_prefetch=5,
        grid=(n_blocks,),
        in_specs=[hbm, hbm, hbm, hbm],
        out_specs=pl.BlockSpec((MOE_BLOCK, d), lambda b, *_: (b, 0)),
        scratch_shapes=([pltpu.VMEM((MOE_BLOCK, d), F32)] * MOE_GATHER_BUFFERS
                        + [pltpu.VMEM((MOE_WEIGHT_SLOTS, d, dff), F32), pltpu.VMEM((MOE_WEIGHT_SLOTS, d, dff), F32),
                           pltpu.VMEM((MOE_WEIGHT_SLOTS, dff, d), F32),
                           pltpu.VMEM((d, dff), BF16), pltpu.VMEM((d, dff), BF16), pltpu.VMEM((dff, d), BF16),
                           pltpu.SemaphoreType.DMA((MOE_GATHER_BUFFERS,)),
                           pltpu.SemaphoreType.DMA((MOE_WEIGHT_SLOTS,))]),
    )
    return pl.pallas_call(
        functools.partial(_moe_kernel, layer=layer),
        grid_spec=grid_spec,
        out_shape=jax.ShapeDtypeStruct((n_slots, d), F32),
        compiler_params=_cparams("arbitrary"),
        name="moe_experts",
    )(block_pos, block_row0, expert_seq, sorted_tokens, counts, h, w_gate, w_up, w_down)


COMBINE_TILE = 256


COMBINE_BUFFERS = 4


def _combine_ln_kernel(pos_ref, y_hbm, h_ref, route_ref, gam_ref, bet_ref, hout_ref, hbf_ref, *scratch):
    t = pl.program_id(0)
    n_t = pl.num_programs(0)
    sems = scratch[-1]
    bufs = [scratch[r * MOE_TOP_K:(r + 1) * MOE_TOP_K] for r in range(COMBINE_BUFFERS)]
    depth = COMBINE_BUFFERS - 1

    def start_gather(tile, r):
        for k in range(MOE_TOP_K):
            _start_row_gather(y_hbm, lambda i: pos_ref[(tile * COMBINE_TILE + i) * MOE_TOP_K + k],
                              bufs[r][k], sems.at[r])

    def wait_gather(r):
        for k in range(MOE_TOP_K):
            _wait_row_gather(bufs[r][k], sems.at[r])

    @pl.when(t == 0)
    def _():
        for r in range(depth):
            start_gather(jnp.minimum(r, n_t - 1), r)

    for r in range(COMBINE_BUFFERS):
        @pl.when(lax.rem(t, COMBINE_BUFFERS) == r)
        def _():
            wait_gather(r)
            start_gather(jnp.minimum(t + depth, n_t - 1), (r + depth) % COMBINE_BUFFERS)
            route = route_ref[...]
            ffn = route[:, 2:3] * bufs[r][0][...] + route[:, 3:4] * bufs[r][1][...]
            hn = _layer_norm(DEEPNORM_ALPHA * h_ref[...] + ffn, gam_ref[...], bet_ref[...])
            hout_ref[...] = hn
            hbf_ref[...] = hn.astype(BF16)

        @pl.when(jnp.logical_and(t == n_t - 1, lax.rem(t, COMBINE_BUFFERS) == r))
        def _():
            for ahead in range(1, COMBINE_BUFFERS):
                wait_gather((r + ahead) % COMBINE_BUFFERS)


def _combine_ln(y_slots, pos, h, route, gamma, beta):
    s, d = h.shape
    tm = COMBINE_TILE
    grid_spec = pltpu.PrefetchScalarGridSpec(
        num_scalar_prefetch=1,
        grid=(s // tm,),
        in_specs=[pl.BlockSpec(memory_space=pl.ANY),
                  pl.BlockSpec((tm, d), lambda i, pos: (i, 0)),
                  pl.BlockSpec((tm, LANES), lambda i, pos: (i, 0)),
                  pl.BlockSpec((1, d), lambda i, pos: (0, 0)),
                  pl.BlockSpec((1, d), lambda i, pos: (0, 0))],
        out_specs=[pl.BlockSpec((tm, d), lambda i, pos: (i, 0)),
                   pl.BlockSpec((tm, d), lambda i, pos: (i, 0))],
        scratch_shapes=([pltpu.VMEM((tm, d), F32)] * (COMBINE_BUFFERS * MOE_TOP_K)
                        + [pltpu.SemaphoreType.DMA((COMBINE_BUFFERS,))]),
    )
    return pl.pallas_call(
        _combine_ln_kernel,
        grid_spec=grid_spec,
        out_shape=[jax.ShapeDtypeStruct((s, d), F32), jax.ShapeDtypeStruct((s, d), BF16)],
        compiler_params=_cparams("arbitrary"),
        name="combine_ln",
    )(pos, y_slots, h, route, gamma, beta)


def _dispatch_tables(route):
    s = route.shape[0]
    n_asg = s * MOE_TOP_K
    expert = route[:, :MOE_TOP_K].astype(jnp.int32).reshape(n_asg)
    onehot = (expert[:, None] == jnp.arange(MOE_EXPERTS, dtype=jnp.int32)[None, :]).astype(jnp.int32)
    csum = jnp.cumsum(onehot, axis=0)
    counts = csum[-1]
    padded = (counts + MOE_BLOCK - 1) // MOE_BLOCK * MOE_BLOCK
    pend = jnp.cumsum(padded)
    pstart = pend - padded
    pos = jnp.sum(onehot * (csum - 1 + pstart[None, :]), axis=1).astype(jnp.int32)
    n_blocks = (n_asg + MOE_EXPERTS * (MOE_BLOCK - 1) + MOE_BLOCK - 1) // MOE_BLOCK
    order = jnp.argsort(expert, stable=True).astype(jnp.int32)
    filler = (jnp.arange(MOE_BLOCK, dtype=jnp.int32) * PAD_ROW_STRIDE) % s
    sorted_tokens = jnp.concatenate([order // MOE_TOP_K, filler])
    block_ids = jnp.arange(n_blocks, dtype=jnp.int32)
    block_start = block_ids * MOE_BLOCK
    block_expert = jnp.sum((pend[None, :] <= block_start[:, None]).astype(jnp.int32), axis=1)
    block_expert = jnp.minimum(block_expert, MOE_EXPERTS - 1).astype(jnp.int32)
    own = block_expert[:, None] == jnp.arange(MOE_EXPERTS, dtype=jnp.int32)[None, :]
    seg_start = jnp.sum(jnp.where(own, (jnp.cumsum(counts) - counts)[None, :], 0), axis=1)
    seg_first_block = jnp.sum(jnp.where(own, (pstart // MOE_BLOCK)[None, :], 0), axis=1)
    block_row0 = (seg_start + (block_ids - seg_first_block) * MOE_BLOCK).astype(jnp.int32)
    present = counts > 0
    seq_pos = jnp.cumsum(present.astype(jnp.int32)) - 1
    n_exp = seq_pos[-1] + 1
    expert_ids = jnp.arange(MOE_EXPERTS, dtype=jnp.int32)
    expert_seq = jnp.zeros((MOE_EXPERTS,), jnp.int32).at[jnp.where(present, seq_pos, MOE_EXPERTS)].set(
        expert_ids, mode='drop')
    block_pos = jnp.sum(jnp.where(block_expert[:, None] == expert_ids[None, :], seq_pos[None, :], 0), axis=1)
    block_pos = jnp.minimum(block_pos, n_exp - 1).astype(jnp.int32)
    counts2 = jnp.stack([pend[-1] // MOE_BLOCK, n_exp]).astype(jnp.int32)
    return pos, sorted_tokens, block_row0, block_pos, expert_seq, counts2


def _moe_layer(h, route, w_gate, w_up, w_down, layer, gamma, beta):
    pos, sorted_tokens, block_row0, block_pos, expert_seq, counts = _dispatch_tables(route)
    y_slots = _moe_experts(h, block_pos, block_row0, expert_seq, sorted_tokens, counts, w_gate, w_up, w_down, layer)
    return _combine_ln(y_slots, pos, h, route, gamma, beta)


def _t5_bucket(dist):
    n = jnp.maximum(dist, 0)
    max_exact = REL_BUCKETS // 2
    ratio = jnp.maximum(n, max_exact).astype(F32) / max_exact
    large = max_exact + (jnp.log(ratio) / math.log(REL_MAX_DIST / max_exact) * (REL_BUCKETS - max_exact)).astype(jnp.int32)
    large = jnp.minimum(large, REL_BUCKETS - 1)
    return jnp.where(n < max_exact, n, large)


def _bias_kernel(idx_ref, tab_ref, o_ref):
    idx = idx_ref[0]
    tab = tab_ref[0]
    acc = jnp.zeros(o_ref.shape[1:], F32)
    for b in range(REL_BUCKETS):
        acc = jnp.where(idx == b, tab[:, b:b + 1], acc)
    p = lax.broadcasted_iota(jnp.int32, acc.shape, 1)
    ik = _mod_pow2(p, 2 * ATT_BLOCK)
    iq = lax.shift_right_logical(p, int(math.log2(2 * ATT_BLOCK)))
    rel = iq + ATT_BLOCK - ik
    o_ref[0] = jnp.where((rel >= 0) & (rel <= ATT_BLOCK), acc, -jnp.inf)


def _attention_bias(rel_bias):
    iq = jnp.arange(ATT_BLOCK)[:, None]
    ik = jnp.arange(2 * ATT_BLOCK)[None, :]
    rel = iq + ATT_BLOCK - ik
    idx = jnp.stack([_t5_bucket(rel * dil) for _, dil in DILATION_CONFIGS], 0)
    n_pairs = ATT_BLOCK * 2 * ATT_BLOCK
    idx = idx.reshape(N_DIL_GROUPS, 1, n_pairs).astype(jnp.int32)
    tab = jnp.transpose(rel_bias.astype(F32), (1, 2, 0))
    bias = pl.pallas_call(
        _bias_kernel,
        grid=(N_DIL_GROUPS,),
        in_specs=[pl.BlockSpec((1, 1, n_pairs), lambda g: (g, 0, 0)),
                  pl.BlockSpec((1, ATT_HEADS, REL_BUCKETS), lambda g: (g, 0, 0))],
        out_specs=pl.BlockSpec((1, ATT_HEADS, n_pairs), lambda g: (g, 0, 0)),
        out_shape=jax.ShapeDtypeStruct((N_DIL_GROUPS, ATT_HEADS, n_pairs), F32),
        compiler_params=_cparams("arbitrary"),
        name="attn_bias",
    )(idx, tab)
    return bias.reshape(N_DIL_GROUPS, ATT_HEADS, ATT_BLOCK, 2 * ATT_BLOCK)


LSE_LANES_PER_HEAD = LANES // ATT_HEADS


def _lse_lane(h):
    return (h & 1) * ATT_HEAD_DIM + (h >> 1) * LSE_LANES_PER_HEAD


def _attn_kernel(q_ref, kc_ref, vc_ref, bias_ref, o_ref, lse_ref, kprev, vprev):
    n = pl.program_id(1)
    pair_w = 2 * ATT_HEAD_DIM
    assert pair_w == LANES and 2 * LSE_LANES_PER_HEAD * (ATT_HEADS // 2) == LANES
    dn = (((1,), (1,)), ((), ()))

    def attend(rows, prev):
        n_keys = (2 if prev else 1) * ATT_BLOCK
        lane_k = lax.broadcasted_iota(jnp.int32, (n_keys, pair_w), 1)
        head_a = lane_k < ATT_HEAD_DIM
        ones_a = jnp.where(head_a, 1.0, 0.0).astype(BF16)
        ones_b = jnp.where(head_a, 0.0, 1.0).astype(BF16)
        lane_q = lax.broadcasted_iota(jnp.int32, (ATT_BLOCK, pair_w), 1)
        zero = jnp.zeros((n_keys, pair_w), BF16)
        lse_parts = []
        for p in range(ATT_HEADS // 2):
            ps = slice(p * pair_w, (p + 1) * pair_w)
            q = q_ref[rows, ps] * (ATT_HEAD_DIM ** -0.5)
            k = kc_ref[rows, ps]
            v = vc_ref[rows, ps]
            if prev:
                k = jnp.concatenate([prev[0][prev[2], ps], k], axis=0)
                v = jnp.concatenate([prev[1][prev[2], ps], v], axis=0)
            k_ab = jnp.concatenate([jnp.where(head_a, k, zero), jnp.where(head_a, zero, k)], axis=0)
            s = lax.dot_general(q, k_ab, dn, preferred_element_type=F32)
            key0 = 0 if prev else ATT_BLOCK
            s_a = s[:, :n_keys] + bias_ref[0, 2 * p][:, key0:]
            s_b = s[:, n_keys:] + bias_ref[0, 2 * p + 1][:, key0:]
            if prev:
                m_a = jnp.max(jnp.maximum(s_a[:, :ATT_BLOCK], s_a[:, ATT_BLOCK:]), axis=-1, keepdims=True)
                m_b = jnp.max(jnp.maximum(s_b[:, :ATT_BLOCK], s_b[:, ATT_BLOCK:]), axis=-1, keepdims=True)
            else:
                m_a = jnp.max(s_a, axis=-1, keepdims=True)
                m_b = jnp.max(s_b, axis=-1, keepdims=True)
            p_a = jnp.exp(s_a - m_a).astype(BF16)
            p_b = jnp.exp(s_b - m_b).astype(BF16)
            rhs_a = jnp.concatenate([jnp.where(head_a, v, zero), ones_a], axis=1)
            rhs_b = jnp.concatenate([jnp.where(head_a, zero, v), ones_b], axis=1)
            acc = (jnp.dot(p_a, rhs_a, preferred_element_type=F32)
                   + jnp.dot(p_b, rhs_b, preferred_element_type=F32))
            l = acc[:, pair_w:]
            o_ref[rows, ps] = (acc[:, :pair_w] / l).astype(o_ref.dtype)
            lse_pair = jnp.where(lane_q < ATT_HEAD_DIM, m_a, m_b) + jnp.log(l)
            in_pair = _floor_to_pow2(_mod_pow2(lane_q, ATT_HEAD_DIM), LSE_LANES_PER_HEAD) == p * LSE_LANES_PER_HEAD
            lse_parts.append(jnp.where(in_pair, lse_pair, 0.0))
        while len(lse_parts) > 1:
            lse_parts = [a + b for a, b in zip(lse_parts[::2], lse_parts[1::2])]
        lse_ref[rows, :] = lse_parts[0]

    def run(first_has_prev):
        for sub in range(ATT_STEP_BLOCKS):
            rows = slice(sub * ATT_BLOCK, (sub + 1) * ATT_BLOCK)
            if sub > 0:
                prev = (kc_ref, vc_ref, slice((sub - 1) * ATT_BLOCK, sub * ATT_BLOCK))
            else:
                prev = (kprev, vprev, slice(None)) if first_has_prev else None
            attend(rows, prev)
        last = slice((ATT_STEP_BLOCKS - 1) * ATT_BLOCK, ATT_STEP_BLOCKS * ATT_BLOCK)
        kprev[...] = kc_ref[last, :]
        vprev[...] = vc_ref[last, :]

    @pl.when(n > 0)
    def _():
        run(True)

    @pl.when(n == 0)
    def _():
        run(False)


def _dilated_branch(qkv, bias, g, dilation):
    s = qkv.shape[0]
    hd = ATT_HEADS * ATT_HEAD_DIM
    sub_len = s // dilation
    step_rows = ATT_STEP_BLOCKS * ATT_BLOCK
    n_steps = sub_len // step_rows
    assert s % dilation == 0 and sub_len % step_rows == 0

    def cur(j):
        return pl.BlockSpec((step_rows, hd), lambda r, n: (r * n_steps + n, j))

    return pl.pallas_call(
        _attn_kernel,
        grid=(dilation, n_steps),
        in_specs=[cur(0), cur(1), cur(2),
                  pl.BlockSpec((1, ATT_HEADS, ATT_BLOCK, 2 * ATT_BLOCK), lambda r, n: (g, 0, 0, 0))],
        out_specs=[pl.BlockSpec((step_rows, hd), lambda r, n: (r * n_steps + n, 0)),
                   pl.BlockSpec((step_rows, LANES), lambda r, n: (r * n_steps + n, 0))],
        out_shape=[jax.ShapeDtypeStruct((s, hd), BF16), jax.ShapeDtypeStruct((s, LANES), F32)],
        scratch_shapes=[pltpu.VMEM((ATT_BLOCK, hd), BF16), pltpu.VMEM((ATT_BLOCK, hd), BF16)],
        compiler_params=_cparams("arbitrary", "arbitrary"),
        name=f"dilated_attn_g{g}",
    )(qkv, qkv, qkv, bias)


def _attn_combine_kernel(o0, o1, o2, l0, l1, l2, out_ref):
    a0, a1, a2 = l0[...], l1[...], l2[...]
    m = jnp.maximum(jnp.maximum(a0, a1), a2)
    e0, e1, e2 = jnp.exp(a0 - m), jnp.exp(a1 - m), jnp.exp(a2 - m)
    inv = 1.0 / (e0 + e1 + e2)
    hd = out_ref.shape[1]
    r = lax.broadcasted_iota(jnp.int32, (LANES, hd), 0)
    c = lax.broadcasted_iota(jnp.int32, (LANES, hd), 1)
    head = lax.shift_right_logical(c, int(math.log2(ATT_HEAD_DIM)))
    widen = jnp.where(r == _lse_lane(head), 1.0, 0.0)
    widen3 = jnp.concatenate([widen.astype(BF16)] * 3, axis=0)
    acc = jnp.zeros(out_ref.shape, F32)
    for e, o in ((e0, o0), (e1, o1), (e2, o2)):
        acc = acc + _dot_01_right(e * inv, widen3) * o[...].astype(F32)
    out_ref[...] = acc.astype(out_ref.dtype)


def _attn_combine(outs, lses, tm=512):
    s, d = outs[0].shape
    spec = pl.BlockSpec((tm, d), lambda i: (i, 0))
    lspec = pl.BlockSpec((tm, LANES), lambda i: (i, 0))
    return pl.pallas_call(
        _attn_combine_kernel,
        grid=(s // tm,),
        in_specs=[spec] * 3 + [lspec] * 3,
        out_specs=spec,
        out_shape=jax.ShapeDtypeStruct((s, d), BF16),
        compiler_params=_cparams("arbitrary"),
        name="attn_combine",
    )(*outs, *lses)


def _to_strided_order(x, dilation):
    s, c = x.shape
    return x.reshape(s // dilation, dilation, c).transpose(1, 0, 2).reshape(s, c)


def _from_strided_order(x, dilation):
    s, c = x.shape
    return x.reshape(dilation, s // dilation, c).transpose(1, 0, 2).reshape(s, c)


def _dilated_attention(h_bf, w_in_bf, bias):
    per_group = 3 * ATT_HEADS * ATT_HEAD_DIM
    outs, lses = [], []
    for g, (_, dilation) in enumerate(DILATION_CONFIGS):
        qkv = _matmul(_to_strided_order(h_bf, dilation), w_in_bf[:, g * per_group:(g + 1) * per_group], BF16)
        o, l = _dilated_branch(qkv, bias, g, dilation)
        outs.append(_from_strided_order(o, dilation))
        lses.append(_from_strided_order(l, dilation))
    return _attn_combine(outs, lses)


def _head_expand(n_heads, width):
    r = lax.broadcasted_iota(jnp.int32, (LANES, n_heads * width), 0)
    c = lax.broadcasted_iota(jnp.int32, (LANES, n_heads * width), 1)
    return jnp.where(_floor_to_pow2(c, width) == r * width, 1.0, 0.0).astype(BF16)


def _causal_conv_silu(cur, halo, w, b):
    rows = lax.broadcasted_iota(jnp.int32, (SUBLANES, cur.shape[1]), 0)
    acc = cur * w[SSM_CONV - 1:SSM_CONV, :] + b
    for j in range(1, SSM_CONV):
        shifted = pltpu.roll(cur, j, 0)
        head = jnp.where(rows < j, pltpu.roll(halo, j, 0), shifted[:SUBLANES])
        shifted = jnp.concatenate([head, shifted[SUBLANES:]], axis=0)
        acc = acc + shifted * w[SSM_CONV - 1 - j:SSM_CONV - j, :]
    return _silu(acc)


def _ssd_kernel(z_ref, x_ref, bc_ref, xh_ref, bch_ref, dt_ref, cwx_ref, cbx_ref, cwbc_ref, cbbc_ref,
                dtb_ref, aneg_ref, dskip_ref, nw_ref, o_ref, state_ref):
    c = pl.program_id(0)

    @pl.when(c == 0)
    def _():
        state_ref[...] = jnp.zeros_like(state_ref)

    first = c == 0
    xh = jnp.where(first, 0.0, xh_ref[...].astype(F32))
    bch = jnp.where(first, 0.0, bch_ref[...].astype(F32))
    xs = _causal_conv_silu(x_ref[...].astype(F32), xh, cwx_ref[...], cbx_ref[...])
    bc = _causal_conv_silu(bc_ref[...].astype(F32), bch, cwbc_ref[...], cbbc_ref[...])
    gn = SSM_GROUPS * SSM_STATE

    x_dt = dt_ref[...] + dtb_ref[...]
    dt = jnp.maximum(x_dt, 0.0) + jnp.log1p(jnp.exp(-jnp.abs(x_dt)))
    a = dt * aneg_ref[...]
    row = lax.broadcasted_iota(jnp.int32, (SSM_CHUNK, SSM_CHUNK), 0)
    colv = lax.broadcasted_iota(jnp.int32, (SSM_CHUNK, SSM_CHUNK), 1)
    causal = colv <= row
    tri = jnp.where(causal, 1.0, 0.0).astype(BF16)
    a_cs = _dot_01_left(jnp.concatenate([tri] * 3, axis=1), a)
    a_cs_t = a_cs.T
    expand = _head_expand(SSM_HEADS, SSM_HEAD_DIM)
    a_last = a_cs[SSM_CHUNK - 1:SSM_CHUNK, :]
    per_head = jnp.concatenate([dt, jnp.exp(a_cs), jnp.exp(a_last - a_cs)], axis=0)
    per_head_e = _dot_01_right(per_head, jnp.concatenate([expand] * 3, axis=0))
    dt_e = per_head_e[:SSM_CHUNK]
    in_decay_e = per_head_e[SSM_CHUNK:2 * SSM_CHUNK]
    out_decay_e = per_head_e[2 * SSM_CHUNK:]
    chunk_decay_e = in_decay_e[SSM_CHUNK - 1:SSM_CHUNK, :]

    xdt = xs * dt_e
    xdt_bf = xdt.astype(BF16)
    xend_bf = (xdt * out_decay_e).astype(BF16)
    gw = SSM_HEADS_PER_GROUP * SSM_HEAD_DIM
    ys = []
    for g in range(SSM_GROUPS):
        b_g = bc[:, g * SSM_STATE:(g + 1) * SSM_STATE]
        c_g = bc[:, gn + g * SSM_STATE:gn + (g + 1) * SSM_STATE].astype(BF16)
        cb = lax.dot_general(c_g, b_g.astype(BF16), (((1,), (1,)), ((), ())), preferred_element_type=F32)
        prev = state_ref[g]
        y_off = jnp.dot(c_g, prev.astype(BF16), preferred_element_type=F32) * in_decay_e[:, g * gw:(g + 1) * gw]
        y_parts = []
        for r in range(SSM_HEADS_PER_GROUP):
            hh = g * SSM_HEADS_PER_GROUP + r
            seg = jnp.exp(jnp.where(causal, a_cs[:, hh:hh + 1] - a_cs_t[hh:hh + 1, :], -jnp.inf))
            y_parts.append(jnp.dot((cb * seg).astype(BF16), xdt_bf[:, hh * SSM_HEAD_DIM:(hh + 1) * SSM_HEAD_DIM],
                                   preferred_element_type=F32))
        ys.append(jnp.concatenate(y_parts, axis=1) + y_off)
        new = jnp.dot(b_g.T.astype(BF16), xend_bf[:, g * gw:(g + 1) * gw], preferred_element_type=F32)
        state_ref[g] = prev * chunk_decay_e[:, g * gw:(g + 1) * gw] + new
    y = jnp.concatenate(ys, axis=1) + xs * dskip_ref[...]
    u = y * _silu(z_ref[...].astype(F32))
    gsz = SSM_D_INNER // SSM_GROUPS
    outs = []
    for g in range(SSM_GROUPS):
        ug = u[:, g * gsz:(g + 1) * gsz]
        outs.append(ug * lax.rsqrt(jnp.mean(ug * ug, axis=-1, keepdims=True) + NORM_EPS))
    o_ref[...] = (jnp.concatenate(outs, axis=1) * nw_ref[...]).astype(o_ref.dtype)


def _mamba2_inner(zxbc, dt_raw, conv_w, conv_b, dt_bias, a_log, d_skip, norm_w):
    s = zxbc.shape[0]
    n_c = s // SSM_CHUNK
    di = SSM_D_INNER
    halo_blocks = SSM_CHUNK // SUBLANES

    def pad_lanes(v):
        return jnp.pad(v.astype(F32), (0, LANES - v.shape[0])).reshape(1, LANES)

    cw = conv_w.astype(F32)
    cb = conv_b.astype(F32).reshape(1, -1)
    dskip = jnp.repeat(d_skip.astype(F32), SSM_HEAD_DIM).reshape(1, di)
    a_neg = pad_lanes(-jnp.exp(a_log.astype(F32)))

    def halo(colblk):
        return pl.BlockSpec((SUBLANES, di), lambda c: (jnp.maximum(c * halo_blocks - 1, 0), colblk))

    full = lambda shape: pl.BlockSpec(shape, lambda c: (0, 0))
    return pl.pallas_call(
        _ssd_kernel,
        grid=(n_c,),
        in_specs=[pl.BlockSpec((SSM_CHUNK, di), lambda c: (c, 0)),
                  pl.BlockSpec((SSM_CHUNK, di), lambda c: (c, 1)),
                  pl.BlockSpec((SSM_CHUNK, SSM_BC_DIM), lambda c: (c, 2)),
                  halo(1), halo(2),
                  pl.BlockSpec((SSM_CHUNK, LANES), lambda c: (c, 0)),
                  full((SSM_CONV, di)), full((1, di)), full((SSM_CONV, SSM_BC_DIM)), full((1, SSM_BC_DIM)),
                  full((1, LANES)), full((1, LANES)), full((1, di)), full((1, di))],
        out_specs=pl.BlockSpec((SSM_CHUNK, di), lambda c: (c, 0)),
        out_shape=jax.ShapeDtypeStruct((s, di), BF16),
        scratch_shapes=[pltpu.VMEM((SSM_GROUPS, SSM_STATE, SSM_HEADS_PER_GROUP * SSM_HEAD_DIM), F32)],
        compiler_params=_cparams("arbitrary"),
        name="ssd_chunk_scan",
    )(zxbc, zxbc, zxbc, zxbc, zxbc, dt_raw, cw[:, :di], cb[:, :di], cw[:, di:], cb[:, di:],
      pad_lanes(dt_bias), a_neg, dskip, norm_w.astype(F32).reshape(1, di))


def _mamba2_mixer(h_bf, w_in, conv_w, conv_b, dt_bias, a_log, d_skip, norm_w):
    n_main = SSM_D_INNER + SSM_D_INNER + SSM_BC_DIM
    assert SSM_BC_DIM == SSM_D_INNER
    w_main = w_in[:, :n_main].astype(BF16)
    w_dt = jnp.pad(w_in[:, n_main:], ((0, 0), (0, LANES - SSM_HEADS))).astype(BF16)
    zxbc = _matmul(h_bf, w_main, BF16)
    dt_raw = _matmul(h_bf, w_dt, F32)
    return _mamba2_inner(zxbc, dt_raw, conv_w, conv_b, dt_bias, a_log, d_skip, norm_w)


def _gla_kernel(q_ref, f_ref, v_ref, gate_ref, lb_ref, nw_ref, o_ref, state_ref):
    c = pl.program_id(0)

    @pl.when(c == 0)
    def _():
        state_ref[...] = jnp.zeros_like(state_ref)

    L = GLA_CHUNK
    K = HGRN_HEAD_DIM
    n_sub = L // GLA_SUB
    row = lax.broadcasted_iota(jnp.int32, (L, L), 0)
    col = lax.broadcasted_iota(jnp.int32, (L, L), 1)
    tri = jnp.where(col <= row, 1.0, 0.0).astype(BF16)
    tri3 = jnp.concatenate([tri] * 3, axis=1)
    sub3 = lax.broadcasted_iota(jnp.int32, (n_sub, GLA_SUB, L), 1)
    rel3 = (lax.broadcasted_iota(jnp.int32, (n_sub, GLA_SUB, L), 2)
            - lax.broadcasted_iota(jnp.int32, (n_sub, GLA_SUB, L), 0) * GLA_SUB)
    rel3 = jnp.where(rel3 >= 0, jnp.where(rel3 <= sub3, rel3, -1), -1)
    levels = []
    m = 2 * GLA_SUB
    while m <= L:
        levels.append(m)
        m *= 2
    same_block = {m: _floor_to_pow2(row, m) == _floor_to_pow2(col, m) for m in levels if m < L}

    for h in range(HGRN_HEADS):
        hs = slice(h * K, (h + 1) * K)
        f = f_ref[:, hs]
        lb = lb_ref[:, hs]
        q = _silu(q_ref[:, hs].astype(F32))
        v_bf = v_ref[:, hs]
        l1p = jnp.log1p(jnp.exp(-jnp.abs(f)))
        log_k = jnp.log1p(-lb) + jnp.minimum(-f, 0.0) - l1p
        ta = jnp.log(lb)
        tb = jnp.log1p(-lb) + jnp.minimum(f, 0.0) - l1p
        log_f = jnp.maximum(ta, tb) + jnp.log1p(jnp.exp(-jnp.abs(ta - tb)))
        g = _dot_01_left(tri3, log_f * LOG2_E)
        gk = g - log_k * LOG2_E

        g3 = g.reshape(n_sub, GLA_SUB, K)
        q3 = q.reshape(n_sub, GLA_SUB, K)
        gk3 = gk.reshape(n_sub, GLA_SUB, K)
        a3 = jnp.zeros((n_sub, GLA_SUB, L), F32)
        for s in range(GLA_SUB):
            e = jnp.exp2(g3 - gk3[:, s:s + 1, :])
            val = jnp.sum(q3 * e, axis=-1, keepdims=True)
            a3 = jnp.where(rel3 == s, val, a3)
        att = a3.reshape(L, L)

        for m in levels:
            half = m // 2
            zeros = jnp.zeros((half, K), F32)
            q_parts, k_parts = [], []
            for j in range(L // m):
                lo, mid, hi = j * m, j * m + half, (j + 1) * m
                gb = g[mid - 1:mid, :]
                q_parts += [zeros, q[mid:hi] * jnp.exp2(g[mid:hi] - gb)]
                k_parts += [jnp.exp2(gb - gk[lo:mid]), zeros]
            ql = jnp.concatenate(q_parts, axis=0).astype(BF16)
            kl = jnp.concatenate(k_parts, axis=0).astype(BF16)
            al = lax.dot_general(ql, kl, (((1,), (1,)), ((), ())), preferred_element_type=F32)
            att = att + (jnp.where(same_block[m], al, 0.0) if m < L else al)

        state_t = state_ref[h]
        o = (jnp.dot(att.astype(BF16), v_bf, preferred_element_type=F32)
             + lax.dot_general((q * jnp.exp2(g)).astype(BF16), state_t.astype(BF16), (((1,), (1,)), ((), ())),
                               preferred_element_type=F32))
        g_last = g[L - 1:L, :]
        k_dec = jnp.exp2(g_last - gk).astype(BF16)
        v_t = v_bf.astype(F32).T.astype(BF16)
        state_ref[h] = state_t * jnp.exp2(g_last) + jnp.dot(v_t, k_dec, preferred_element_type=F32)
        o = o * lax.rsqrt(jnp.mean(o * o, axis=-1, keepdims=True) + NORM_EPS) * nw_ref[...]
        o_ref[:, hs] = (o * _silu(gate_ref[:, hs].astype(F32))).astype(o_ref.dtype)


def _gla(qig, f, lb, norm_w):
    s, d = f.shape
    n_c = s // GLA_CHUNK
    return pl.pallas_call(
        _gla_kernel,
        grid=(n_c,),
        in_specs=[pl.BlockSpec((GLA_CHUNK, d), lambda c: (c, 0)),
                  pl.BlockSpec((GLA_CHUNK, d), lambda c: (c, 0)),
                  pl.BlockSpec((GLA_CHUNK, d), lambda c: (c, 1)),
                  pl.BlockSpec((GLA_CHUNK, d), lambda c: (c, 2)),
                  pl.BlockSpec((1, d), lambda c: (0, 0)),
                  pl.BlockSpec((1, HGRN_HEAD_DIM), lambda c: (0, 0))],
        out_specs=pl.BlockSpec((GLA_CHUNK, d), lambda c: (c, 0)),
        out_shape=jax.ShapeDtypeStruct((s, d), BF16),
        scratch_shapes=[pltpu.VMEM((HGRN_HEADS, HGRN_HEAD_DIM, HGRN_HEAD_DIM), F32)],
        compiler_params=_cparams("arbitrary"),
        name="gla_chunk_scan",
    )(qig, f, qig, qig, lb, norm_w)


def _hgrn2_mixer(h_bf, w_in, lb, norm_w):
    d = D_MODEL
    w_q, w_f, w_i, w_g = (w_in[:, j * d:(j + 1) * d] for j in range(4))
    qig = _matmul(h_bf, jnp.concatenate([w_q, w_i, w_g], axis=1).astype(BF16), BF16)
    f = _matmul(h_bf, w_f.astype(BF16), F32)
    return _gla(qig, f, lb.astype(F32).reshape(1, d), norm_w.astype(F32).reshape(1, HGRN_HEAD_DIM))


def kernel(x, attn_w_in, attn_w_out, rel_bias, ssm_w_in, ssm_conv_w, ssm_conv_b, ssm_dt_bias, ssm_a_log, ssm_d, ssm_norm_w, ssm_w_out, hgrn_w_in, hgrn_lower_bound, hgrn_norm_w, hgrn_w_out, moe_w_coarse, moe_w_fine, moe_w_gate, moe_w_up, moe_w_down, ln_gamma, ln_beta):
    b_, s_, d_ = x.shape
    assert b_ == 1 and d_ == D_MODEL
    lbs = jax.nn.softmax(hgrn_lower_bound.astype(F32), axis=0)
    lbs = jnp.cumsum(lbs, axis=0) - lbs[0]
    bias = _attention_bias(rel_bias)
    h = x.reshape(s_, d_).astype(F32)
    h_bf = h.astype(BF16)
    for layer in range(DEPTH):
        kind = layer % N_MIXERS
        j = layer // N_MIXERS
        if kind == 0:
            a = _dilated_attention(h_bf, attn_w_in[j].astype(BF16), bias)
            w_out = attn_w_out[j]
        elif kind == 1:
            a = _mamba2_mixer(h_bf, ssm_w_in[j], ssm_conv_w[j], ssm_conv_b[j], ssm_dt_bias[j], ssm_a_log[j],
                              ssm_d[j], ssm_norm_w[j])
            w_out = ssm_w_out[j]
        else:
            a = _hgrn2_mixer(h_bf, hgrn_w_in[j], lbs[layer], hgrn_norm_w[j])
            w_out = hgrn_w_out[j]
        gam = ln_gamma[layer].astype(F32)
        bet = ln_beta[layer].astype(F32)
        w_route = _router_weights(moe_w_coarse[layer], moe_w_fine[layer])
        h, route = _mix_ln_route(a, w_out.astype(BF16), h, gam[0:1], bet[0:1], w_route)
        h, h_bf = _moe_layer(h, route, moe_w_gate.astype(F32), moe_w_up.astype(F32), moe_w_down.astype(F32), layer,
                             gam[1:2], bet[1:2])
    return h.reshape(b_, s_, d_)
```

```python
import functools
import math

import jax
import jax.numpy as jnp
from jax import lax
from jax.experimental import pallas as pl
from jax.experimental.pallas import tpu as pltpu

F32 = jnp.float32
BF16 = jnp.bfloat16

D_MODEL = 1024
DEPTH = 4
N_MIXERS = 3
DEEPNORM_ALPHA = (2 * DEPTH) ** 0.25
NORM_EPS = 1e-5
LOG2_E = 1.4426950408889634

ATT_HEAD_DIM = 64
ATT_HEADS = D_MODEL // ATT_HEAD_DIM
DILATION_CONFIGS = ((128, 1), (512, 4), (2048, 16))
N_DIL_GROUPS = len(DILATION_CONFIGS)
ATT_BLOCK = 128
ATT_STEP_BLOCKS = 8
REL_BUCKETS = 32
REL_MAX_DIST = 2048

SSM_D_INNER = 2 * D_MODEL
SSM_HEAD_DIM = 64
SSM_HEADS = SSM_D_INNER // SSM_HEAD_DIM
SSM_GROUPS = 8
SSM_HEADS_PER_GROUP = SSM_HEADS // SSM_GROUPS
SSM_STATE = 128
SSM_CONV = 4
SSM_CHUNK = 128
SSM_BC_DIM = 2 * SSM_GROUPS * SSM_STATE

HGRN_HEAD_DIM = 128
HGRN_HEADS = D_MODEL // HGRN_HEAD_DIM
GLA_CHUNK = 128
GLA_SUB = 8

MOE_GROUPS = 8
MOE_EXPERTS_PER_GROUP = 8
MOE_EXPERTS = MOE_GROUPS * MOE_EXPERTS_PER_GROUP
MOE_TOP_K = 2
MOE_D_FF = 512
MOE_BLOCK = 256

LANES = 128
SUBLANES = 8
VMEM_LIMIT_BYTES = 52 * 1024 * 1024


def _cparams(*sem):
    return pltpu.CompilerParams(dimension_semantics=sem, vmem_limit_bytes=VMEM_LIMIT_BYTES)


def _silu(x):
    hx = 0.5 * x
    return hx + hx * jnp.tanh(hx)


def _split_bf16(v, parts):
    out = []
    for _ in range(parts - 1):
        p = v.astype(BF16)
        out.append(p)
        v = v - p.astype(F32)
    out.append(v.astype(BF16))
    return out


def _dot_01_left(sel3, v):
    return jnp.dot(sel3, jnp.concatenate(_split_bf16(v, 3), axis=0), preferred_element_type=F32)


def _dot_01_right(v, sel3):
    return jnp.dot(jnp.concatenate(_split_bf16(v, 3), axis=1), sel3, preferred_element_type=F32)


def _floor_to_pow2(v, m):
    assert m & (m - 1) == 0
    return jnp.bitwise_and(v, -m)


def _mod_pow2(v, m):
    assert m & (m - 1) == 0
    return jnp.bitwise_and(v, m - 1)


def _mm_kernel(x_ref, w_ref, o_ref):
    o_ref[...] = jnp.dot(x_ref[...], w_ref[...], preferred_element_type=F32).astype(o_ref.dtype)


def _matmul(x, w, out_dtype, tm=2048, tn=1024):
    m, k = x.shape
    n = w.shape[1]
    tm = min(tm, m)
    tn = min(tn, n)
    assert m % tm == 0 and n % tn == 0
    return pl.pallas_call(
        _mm_kernel,
        grid=(n // tn, m // tm),
        in_specs=[pl.BlockSpec((tm, k), lambda j, i: (i, 0)),
                  pl.BlockSpec((k, tn), lambda j, i: (0, j))],
        out_specs=pl.BlockSpec((tm, tn), lambda j, i: (i, j)),
        out_shape=jax.ShapeDtypeStruct((m, n), out_dtype),
        compiler_params=_cparams("arbitrary", "arbitrary"),
        name="mm",
    )(x, w)


def _layer_norm(y, gamma, beta):
    mu = jnp.mean(y, axis=-1, keepdims=True)
    yc = y - mu
    var = jnp.mean(yc * yc, axis=-1, keepdims=True)
    return yc * lax.rsqrt(var + NORM_EPS) * gamma + beta


def _mix_ln_route_kernel(a_ref, w_ref, h_ref, gam_ref, bet_ref, wr_ref, hout_ref, route_ref):
    mix = jnp.dot(a_ref[...], w_ref[...], preferred_element_type=F32)
    hn = _layer_norm(DEEPNORM_ALPHA * h_ref[...] + mix, gam_ref[...], bet_ref[...])
    hout_ref[...] = hn
    hn_hi, hn_lo = _split_bf16(hn, 2)
    both = jnp.dot(hn_hi, wr_ref[...], preferred_element_type=F32)
    logits = (both[:, :LANES] + both[:, LANES:]
              + jnp.dot(hn_lo, wr_ref[:, :LANES], preferred_element_type=F32))
    lane = lax.broadcasted_iota(jnp.int32, logits.shape, 1).astype(F32)
    neg = -jnp.inf
    l1 = jnp.where(lane < MOE_GROUPS, logits, neg)
    m1 = jnp.max(l1, axis=-1, keepdims=True)
    grp = jnp.min(jnp.where(l1 == m1, lane, float(LANES)), axis=-1, keepdims=True)
    g1 = 1.0 / jnp.sum(jnp.exp(l1 - m1), axis=-1, keepdims=True)
    lo = MOE_GROUPS + grp * MOE_EXPERTS_PER_GROUP
    l2 = jnp.where(lane >= lo, jnp.where(lane < lo + MOE_EXPERTS_PER_GROUP, logits, neg), neg)
    t1 = jnp.max(l2, axis=-1, keepdims=True)
    i1 = jnp.min(jnp.where(l2 == t1, lane, float(LANES)), axis=-1, keepdims=True)
    l2b = jnp.where(lane == i1, neg, l2)
    t2 = jnp.max(l2b, axis=-1, keepdims=True)
    i2 = jnp.min(jnp.where(l2b == t2, lane, float(LANES)), axis=-1, keepdims=True)
    e21 = jnp.exp(t2 - t1)
    ga = g1 / (1.0 + e21)
    gb = g1 * e21 / (1.0 + e21)
    out = jnp.where(lane == 0, i1 - MOE_GROUPS,
                    jnp.where(lane == 1, i2 - MOE_GROUPS,
                              jnp.where(lane == 2, ga, jnp.where(lane == 3, gb, 0.0))))
    route_ref[...] = out


def _router_weights(w_coarse, w_fine):
    w = jnp.pad(jnp.concatenate([w_coarse, w_fine], axis=1).astype(F32),
                ((0, 0), (0, LANES - MOE_GROUPS - MOE_EXPERTS)))
    hi = w.astype(BF16)
    lo = (w - hi.astype(F32)).astype(BF16)
    return jnp.concatenate([hi, lo], axis=1)


def _mix_ln_route(a, w, h, gamma, beta, w_route, tm=512):
    s, k = a.shape
    d = w.shape[1]
    return pl.pallas_call(
        _mix_ln_route_kernel,
        grid=(s // tm,),
        in_specs=[pl.BlockSpec((tm, k), lambda i: (i, 0)),
                  pl.BlockSpec((k, d), lambda i: (0, 0)),
                  pl.BlockSpec((tm, d), lambda i: (i, 0)),
                  pl.BlockSpec((1, d), lambda i: (0, 0)),
                  pl.BlockSpec((1, d), lambda i: (0, 0)),
                  pl.BlockSpec((d, 2 * LANES), lambda i: (0, 0))],
        out_specs=[pl.BlockSpec((tm, d), lambda i: (i, 0)),
                   pl.BlockSpec((tm, LANES), lambda i: (i, 0))],
        out_shape=[jax.ShapeDtypeStruct((s, d), F32),
                   jax.ShapeDtypeStruct((s, LANES), F32)],
        compiler_params=_cparams("arbitrary"),
        name="mix_ln_route",
    )(a, w, h, gamma, beta, w_route)


def _start_row_gather(src_hbm, row_of, dst_vmem, sem):
    for i in range(dst_vmem.shape[0]):
        pltpu.make_async_copy(src_hbm.at[pl.ds(row_of(i), 1), :], dst_vmem.at[pl.ds(i, 1), :], sem).start(
            priority=i % 2)


def _wait_row_gather(dst_vmem, sem):
    pltpu.make_async_copy(dst_vmem, dst_vmem, sem).wait()


PAD_ROW_STRIDE = 61
MOE_GATHER_BUFFERS = 4
MOE_WEIGHT_SLOTS = 3


def _moe_kernel(bpos_ref, brow_ref, eseq_ref, tok_ref, cnt_ref, h_hbm, wg_hbm, wu_hbm, wd_hbm, y_ref,
                xbuf0, xbuf1, xbuf2, xbuf3, wg_f, wu_f, wd_f, wg_bf, wu_bf, wd_bf, gsems, wsems, *, layer):
    b = pl.program_id(0)
    n_used = cnt_ref[0]
    n_exp = cnt_ref[1]
    bufs = (xbuf0, xbuf1, xbuf2, xbuf3)
    depth = MOE_GATHER_BUFFERS - 1

    def start_gather(blk, r):
        row0 = brow_ref[blk]
        _start_row_gather(h_hbm, lambda i: tok_ref[row0 + i], bufs[r], gsems.at[r])

    def weight_copies(p):
        e = eseq_ref[p]
        slot = lax.rem(p, MOE_WEIGHT_SLOTS)
        return [pltpu.make_async_copy(w.at[layer, e], f.at[slot], wsems.at[slot])
                for w, f in ((wg_hbm, wg_f), (wu_hbm, wu_f), (wd_hbm, wd_f))]

    @pl.when(b == 0)
    def _():
        for r in range(depth):
            start_gather(jnp.minimum(r, n_used - 1), r)
        for c in weight_copies(0):
            c.start()

    @pl.when(jnp.logical_and(b == 0, n_exp > 1))
    def _():
        for c in weight_copies(1):
            c.start()

    p = bpos_ref[b]
    changed = jnp.logical_or(b == 0, p != bpos_ref[jnp.maximum(b - 1, 0)])

    @pl.when(jnp.logical_and(changed, p + 2 < n_exp))
    def _():
        for c in weight_copies(p + 2):
            c.start()

    @pl.when(changed)
    def _():
        for c in weight_copies(p):
            c.wait()
        slot = lax.rem(p, MOE_WEIGHT_SLOTS)
        wg_bf[...] = wg_f[slot].astype(BF16)
        wu_bf[...] = wu_f[slot].astype(BF16)
        wd_bf[...] = wd_f[slot].astype(BF16)

    for r in range(MOE_GATHER_BUFFERS):
        @pl.when(jnp.logical_and(b < n_used, lax.rem(b, MOE_GATHER_BUFFERS) == r))
        def _():
            _wait_row_gather(bufs[r], gsems.at[r])
            start_gather(jnp.minimum(b + depth, n_used - 1), (r + depth) % MOE_GATHER_BUFFERS)
            x = bufs[r][...].astype(BF16)
            g = jnp.dot(x, wg_bf[...], preferred_element_type=F32)
            u = jnp.dot(x, wu_bf[...], preferred_element_type=F32)
            hid = (_silu(g) * u).astype(BF16)
            y_ref[...] = jnp.dot(hid, wd_bf[...], preferred_element_type=F32)

        @pl.when(jnp.logical_and(b == n_used - 1, lax.rem(b, MOE_GATHER_BUFFERS) == r))
        def _():
            for ahead in range(1, MOE_GATHER_BUFFERS):
                rr = (r + ahead) % MOE_GATHER_BUFFERS
                _wait_row_gather(bufs[rr], gsems.at[rr])

    @pl.when(b >= n_used)
    def _():
        y_ref[...] = jnp.zeros_like(y_ref)


def _moe_experts(h, block_pos, block_row0, expert_seq, sorted_tokens, counts, w_gate, w_up, w_down, layer):
    s, d = h.shape
    n_blocks = block_pos.shape[0]
    n_slots = n_blocks * MOE_BLOCK
    dff = w_gate.shape[3]
    hbm = pl.BlockSpec(memory_space=pl.ANY)
    grid_spec = pltpu.PrefetchScalarGridSpec(
        num_scalar_prefetch=5,
        grid=(n_blocks,),
        in_specs=[hbm, hbm, hbm, hbm],
        out_specs=pl.BlockSpec((MOE_BLOCK, d), lambda b, *_: (b, 0)),
        scratch_shapes=([pltpu.VMEM((MOE_BLOCK, d), F32)] * MOE_GATHER_BUFFERS
                        + [pltpu.VMEM((MOE_WEIGHT_SLOTS, d, dff), F32), pltpu.VMEM((MOE_WEIGHT_SLOTS, d, dff), F32),
                           pltpu.VMEM((MOE_WEIGHT_SLOTS, dff, d), F32),
                           pltpu.VMEM((d, dff), BF16), pltpu.VMEM((d, dff), BF16), pltpu.VMEM((dff, d), BF16),
                           pltpu.SemaphoreType.DMA((MOE_GATHER_BUFFERS,)),
                           pltpu.SemaphoreType.DMA((MOE_WEIGHT_SLOTS,))]),
    )
    return pl.pallas_call(
        functools.partial(_moe_kernel, layer=layer),
        grid_spec=grid_spec,
        out_shape=jax.ShapeDtypeStruct((n_slots, d), F32),
        compiler_params=_cparams("arbitrary"),
        name="moe_experts",
    )(block_pos, block_row0, expert_seq, sorted_tokens, counts, h, w_gate, w_up, w_down)


COMBINE_TILE = 256


COMBINE_BUFFERS = 4


def _combine_ln_kernel(pos_ref, y_hbm, h_ref, route_ref, gam_ref, bet_ref, hout_ref, hbf_ref, *scratch):
    t = pl.program_id(0)
    n_t = pl.num_programs(0)
    sems = scratch[-1]
    bufs = [scratch[r * MOE_TOP_K:(r + 1) * MOE_TOP_K] for r in range(COMBINE_BUFFERS)]
    depth = COMBINE_BUFFERS - 1

    def start_gather(tile, r):
        for k in range(MOE_TOP_K):
            _start_row_gather(y_hbm, lambda i: pos_ref[(tile * COMBINE_TILE + i) * MOE_TOP_K + k],
                              bufs[r][k], sems.at[r])

    def wait_gather(r):
        for k in range(MOE_TOP_K):
            _wait_row_gather(bufs[r][k], sems.at[r])

    @pl.when(t == 0)
    def _():
        for r in range(depth):
            start_gather(jnp.minimum(r, n_t - 1), r)

    for r in range(COMBINE_BUFFERS):
        @pl.when(lax.rem(t, COMBINE_BUFFERS) == r)
        def _():
            wait_gather(r)
            start_gather(jnp.minimum(t + depth, n_t - 1), (r + depth) % COMBINE_BUFFERS)
            route = route_ref[...]
            ffn = route[:, 2:3] * bufs[r][0][...] + route[:, 3:4] * bufs[r][1][...]
            hn = _layer_norm(DEEPNORM_ALPHA * h_ref[...] + ffn, gam_ref[...], bet_ref[...])
            hout_ref[...] = hn
            hbf_ref[...] = hn.astype(BF16)

        @pl.when(jnp.logical_and(t == n_t - 1, lax.rem(t, COMBINE_BUFFERS) == r))
        def _():
            for ahead in range(1, COMBINE_BUFFERS):
                wait_gather((r + ahead) % COMBINE_BUFFERS)


def _combine_ln(y_slots, pos, h, route, gamma, beta):
    s, d = h.shape
    tm = COMBINE_TILE
    grid_spec = pltpu.PrefetchScalarGridSpec(
        num_scalar_prefetch=1,
        grid=(s // tm,),
        in_specs=[pl.BlockSpec(memory_space=pl.ANY),
                  pl.BlockSpec((tm, d), lambda i, pos: (i, 0)),
                  pl.BlockSpec((tm, LANES), lambda i, pos: (i, 0)),
                  pl.BlockSpec((1, d), lambda i, pos: (0, 0)),
                  pl.BlockSpec((1, d), lambda i, pos: (0, 0))],
        out_specs=[pl.BlockSpec((tm, d), lambda i, pos: (i, 0)),
                   pl.BlockSpec((tm, d), lambda i, pos: (i, 0))],
        scratch_shapes=([pltpu.VMEM((tm, d), F32)] * (COMBINE_BUFFERS * MOE_TOP_K)
                        + [pltpu.SemaphoreType.DMA((COMBINE_BUFFERS,))]),
    )
    return pl.pallas_call(
        _combine_ln_kernel,
        grid_spec=grid_spec,
        out_shape=[jax.ShapeDtypeStruct((s, d), F32), jax.ShapeDtypeStruct((s, d), BF16)],
        compiler_params=_cparams("arbitrary"),
        name="combine_ln",
    )(pos, y_slots, h, route, gamma, beta)


def _dispatch_tables(route):
    s = route.shape[0]
    n_asg = s * MOE_TOP_K
    expert = route[:, :MOE_TOP_K].astype(jnp.int32).reshape(n_asg)
    onehot = (expert[:, None] == jnp.arange(MOE_EXPERTS, dtype=jnp.int32)[None, :]).astype(jnp.int32)
    csum = jnp.cumsum(onehot, axis=0)
    counts = csum[-1]
    padded = (counts + MOE_BLOCK - 1) // MOE_BLOCK * MOE_BLOCK
    pend = jnp.cumsum(padded)
    pstart = pend - padded
    pos = jnp.sum(onehot * (csum - 1 + pstart[None, :]), axis=1).astype(jnp.int32)
    n_blocks = (n_asg + MOE_EXPERTS * (MOE_BLOCK - 1) + MOE_BLOCK - 1) // MOE_BLOCK
    order = jnp.argsort(expert, stable=True).astype(jnp.int32)
    filler = (jnp.arange(MOE_BLOCK, dtype=jnp.int32) * PAD_ROW_STRIDE) % s
    sorted_tokens = jnp.concatenate([order // MOE_TOP_K, filler])
    block_ids = jnp.arange(n_blocks, dtype=jnp.int32)
    block_start = block_ids * MOE_BLOCK
    block_expert = jnp.sum((pend[None, :] <= block_start[:, None]).astype(jnp.int32), axis=1)
    block_expert = jnp.minimum(block_expert, MOE_EXPERTS - 1).astype(jnp.int32)
    own = block_expert[:, None] == jnp.arange(MOE_EXPERTS, dtype=jnp.int32)[None, :]
    seg_start = jnp.sum(jnp.where(own, (jnp.cumsum(counts) - counts)[None, :], 0), axis=1)
    seg_first_block = jnp.sum(jnp.where(own, (pstart // MOE_BLOCK)[None, :], 0), axis=1)
    block_row0 = (seg_start + (block_ids - seg_first_block) * MOE_BLOCK).astype(jnp.int32)
    present = counts > 0
    seq_pos = jnp.cumsum(present.astype(jnp.int32)) - 1
    n_exp = seq_pos[-1] + 1
    expert_ids = jnp.arange(MOE_EXPERTS, dtype=jnp.int32)
    expert_seq = jnp.zeros((MOE_EXPERTS,), jnp.int32).at[jnp.where(present, seq_pos, MOE_EXPERTS)].set(
        expert_ids, mode='drop')
    block_pos = jnp.sum(jnp.where(block_expert[:, None] == expert_ids[None, :], seq_pos[None, :], 0), axis=1)
    block_pos = jnp.minimum(block_pos, n_exp - 1).astype(jnp.int32)
    counts2 = jnp.stack([pend[-1] // MOE_BLOCK, n_exp]).astype(jnp.int32)
    return pos, sorted_tokens, block_row0, block_pos, expert_seq, counts2


def _moe_layer(h, route, w_gate, w_up, w_down, layer, gamma, beta):
    pos, sorted_tokens, block_row0, block_pos, expert_seq, counts = _dispatch_tables(route)
    y_slots = _moe_experts(h, block_pos, block_row0, expert_seq, sorted_tokens, counts, w_gate, w_up, w_down, layer)
    return _combine_ln(y_slots, pos, h, route, gamma, beta)


def _t5_bucket(dist):
    n = jnp.maximum(dist, 0)
    max_exact = REL_BUCKETS // 2
    ratio = jnp.maximum(n, max_exact).astype(F32) / max_exact
    large = max_exact + (jnp.log(ratio) / math.log(REL_MAX_DIST / max_exact) * (REL_BUCKETS - max_exact)).astype(jnp.int32)
    large = jnp.minimum(large, REL_BUCKETS - 1)
    return jnp.where(n < max_exact, n, large)


def _bias_kernel(idx_ref, tab_ref, o_ref):
    idx = idx_ref[0]
    tab = tab_ref[0]
    acc = jnp.zeros(o_ref.shape[1:], F32)
    for b in range(REL_BUCKETS):
        acc = jnp.where(idx == b, tab[:, b:b + 1], acc)
    p = lax.broadcasted_iota(jnp.int32, acc.shape, 1)
    ik = _mod_pow2(p, 2 * ATT_BLOCK)
    iq = lax.shift_right_logical(p, int(math.log2(2 * ATT_BLOCK)))
    rel = iq + ATT_BLOCK - ik
    o_ref[0] = jnp.where((rel >= 0) & (rel <= ATT_BLOCK), acc, -jnp.inf)


def _attention_bias(rel_bias):
    iq = jnp.arange(ATT_BLOCK)[:, None]
    ik = jnp.arange(2 * ATT_BLOCK)[None, :]
    rel = iq + ATT_BLOCK - ik
    idx = jnp.stack([_t5_bucket(rel * dil) for _, dil in DILATION_CONFIGS], 0)
    n_pairs = ATT_BLOCK * 2 * ATT_BLOCK
    idx = idx.reshape(N_DIL_GROUPS, 1, n_pairs).astype(jnp.int32)
    tab = jnp.transpose(rel_bias.astype(F32), (1, 2, 0))
    bias = pl.pallas_call(
        _bias_kernel,
        grid=(N_DIL_GROUPS,),
        in_specs=[pl.BlockSpec((1, 1, n_pairs), lambda g: (g, 0, 0)),
                  pl.BlockSpec((1, ATT_HEADS, REL_BUCKETS), lambda g: (g, 0, 0))],
        out_specs=pl.BlockSpec((1, ATT_HEADS, n_pairs), lambda g: (g, 0, 0)),
        out_shape=jax.ShapeDtypeStruct((N_DIL_GROUPS, ATT_HEADS, n_pairs), F32),
        compiler_params=_cparams("arbitrary"),
        name="attn_bias",
    )(idx, tab)
    return bias.reshape(N_DIL_GROUPS, ATT_HEADS, ATT_BLOCK, 2 * ATT_BLOCK)


LSE_LANES_PER_HEAD = LANES // ATT_HEADS


def _lse_lane(h):
    return (h & 1) * ATT_HEAD_DIM + (h >> 1) * LSE_LANES_PER_HEAD


def _attn_kernel(q_ref, kc_ref, vc_ref, bias_ref, o_ref, lse_ref, kprev, vprev):
    n = pl.program_id(1)
    pair_w = 2 * ATT_HEAD_DIM
    assert pair_w == LANES and 2 * LSE_LANES_PER_HEAD * (ATT_HEADS // 2) == LANES
    dn = (((1,), (1,)), ((), ()))

    def attend(rows, prev):
        n_keys = (2 if prev else 1) * ATT_BLOCK
        lane_k = lax.broadcasted_iota(jnp.int32, (n_keys, pair_w), 1)
        head_a = lane_k < ATT_HEAD_DIM
        ones_a = jnp.where(head_a, 1.0, 0.0).astype(BF16)
        ones_b = jnp.where(head_a, 0.0, 1.0).astype(BF16)
        lane_q = lax.broadcasted_iota(jnp.int32, (ATT_BLOCK, pair_w), 1)
        zero = jnp.zeros((n_keys, pair_w), BF16)
        lse_parts = []
        for p in range(ATT_HEADS // 2):
            ps = slice(p * pair_w, (p + 1) * pair_w)
            q = q_ref[rows, ps] * (ATT_HEAD_DIM ** -0.5)
            k = kc_ref[rows, ps]
            v = vc_ref[rows, ps]
            if prev:
                k = jnp.concatenate([prev[0][prev[2], ps], k], axis=0)
                v = jnp.concatenate([prev[1][prev[2], ps], v], axis=0)
            k_ab = jnp.concatenate([jnp.where(head_a, k, zero), jnp.where(head_a, zero, k)], axis=0)
            s = lax.dot_general(q, k_ab, dn, preferred_element_type=F32)
            key0 = 0 if prev else ATT_BLOCK
            s_a = s[:, :n_keys] + bias_ref[0, 2 * p][:, key0:]
            s_b = s[:, n_keys:] + bias_ref[0, 2 * p + 1][:, key0:]
            if prev:
                m_a = jnp.max(jnp.maximum(s_a[:, :ATT_BLOCK], s_a[:, ATT_BLOCK:]), axis=-1, keepdims=True)
                m_b = jnp.max(jnp.maximum(s_b[:, :ATT_BLOCK], s_b[:, ATT_BLOCK:]), axis=-1, keepdims=True)
            else:
                m_a = jnp.max(s_a, axis=-1, keepdims=True)
                m_b = jnp.max(s_b, axis=-1, keepdims=True)
            p_a = jnp.exp(s_a - m_a).astype(BF16)
            p_b = jnp.exp(s_b - m_b).astype(BF16)
            rhs_a = jnp.concatenate([jnp.where(head_a, v, zero), ones_a], axis=1)
            rhs_b = jnp.concatenate([jnp.where(head_a, zero, v), ones_b], axis=1)
            acc = (jnp.dot(p_a, rhs_a, preferred_element_type=F32)
                   + jnp.dot(p_b, rhs_b, preferred_element_type=F32))
            l = acc[:, pair_w:]
            o_ref[rows, ps] = (acc[:, :pair_w] / l).astype(o_ref.dtype)
            lse_pair = jnp.where(lane_q < ATT_HEAD_DIM, m_a, m_b) + jnp.log(l)
            in_pair = _floor_to_pow2(_mod_pow2(lane_q, ATT_HEAD_DIM), LSE_LANES_PER_HEAD) == p * LSE_LANES_PER_HEAD
            lse_parts.append(jnp.where(in_pair, lse_pair, 0.0))
        while len(lse_parts) > 1:
            lse_parts = [a + b for a, b in zip(lse_parts[::2], lse_parts[1::2])]
        lse_ref[rows, :] = lse_parts[0]

    def run(first_has_prev):
        for sub in range(ATT_STEP_BLOCKS):
            rows = slice(sub * ATT_BLOCK, (sub + 1) * ATT_BLOCK)
            if sub > 0:
                prev = (kc_ref, vc_ref, slice((sub - 1) * ATT_BLOCK, sub * ATT_BLOCK))
            else:
                prev = (kprev, vprev, slice(None)) if first_has_prev else None
            attend(rows, prev)
        last = slice((ATT_STEP_BLOCKS - 1) * ATT_BLOCK, ATT_STEP_BLOCKS * ATT_BLOCK)
        kprev[...] = kc_ref[last, :]
        vprev[...] = vc_ref[last, :]

    @pl.when(n > 0)
    def _():
        run(True)

    @pl.when(n == 0)
    def _():
        run(False)


def _dilated_branch(qkv, bias, g, dilation):
    s = qkv.shape[0]
    hd = ATT_HEADS * ATT_HEAD_DIM
    sub_len = s // dilation
    step_rows = ATT_STEP_BLOCKS * ATT_BLOCK
    n_steps = sub_len // step_rows
    assert s % dilation == 0 and sub_len % step_rows == 0

    def cur(j):
        return pl.BlockSpec((step_rows, hd), lambda r, n: (r * n_steps + n, j))

    return pl.pallas_call(
        _attn_kernel,
        grid=(dilation, n_steps),
        in_specs=[cur(0), cur(1), cur(2),
                  pl.BlockSpec((1, ATT_HEADS, ATT_BLOCK, 2 * ATT_BLOCK), lambda r, n: (g, 0, 0, 0))],
        out_specs=[pl.BlockSpec((step_rows, hd), lambda r, n: (r * n_steps + n, 0)),
                   pl.BlockSpec((step_rows, LANES), lambda r, n: (r * n_steps + n, 0))],
        out_shape=[jax.ShapeDtypeStruct((s, hd), BF16), jax.ShapeDtypeStruct((s, LANES), F32)],
        scratch_shapes=[pltpu.VMEM((ATT_BLOCK, hd), BF16), pltpu.VMEM((ATT_BLOCK, hd), BF16)],
        compiler_params=_cparams("arbitrary", "arbitrary"),
        name=f"dilated_attn_g{g}",
    )(qkv, qkv, qkv, bias)


def _attn_combine_kernel(o0, o1, o2, l0, l1, l2, out_ref):
    a0, a1, a2 = l0[...], l1[...], l2[...]
    m = jnp.maximum(jnp.maximum(a0, a1), a2)
    e0, e1, e2 = jnp.exp(a0 - m), jnp.exp(a1 - m), jnp.exp(a2 - m)
    inv = 1.0 / (e0 + e1 + e2)
    hd = out_ref.shape[1]
    r = lax.broadcasted_iota(jnp.int32, (LANES, hd), 0)
    c = lax.broadcasted_iota(jnp.int32, (LANES, hd), 1)
    head = lax.shift_right_logical(c, int(math.log2(ATT_HEAD_DIM)))
    widen = jnp.where(r == _lse_lane(head), 1.0, 0.0)
    widen3 = jnp.concatenate([widen.astype(BF16)] * 3, axis=0)
    acc = jnp.zeros(out_ref.shape, F32)
    for e, o in ((e0, o0), (e1, o1), (e2, o2)):
        acc = acc + _dot_01_right(e * inv, widen3) * o[...].astype(F32)
    out_ref[...] = acc.astype(out_ref.dtype)


def _attn_combine(outs, lses, tm=512):
    s, d = outs[0].shape
    spec = pl.BlockSpec((tm, d), lambda i: (i, 0))
    lspec = pl.BlockSpec((tm, LANES), lambda i: (i, 0))
    return pl.pallas_call(
        _attn_combine_kernel,
        grid=(s // tm,),
        in_specs=[spec] * 3 + [lspec] * 3,
        out_specs=spec,
        out_shape=jax.ShapeDtypeStruct((s, d), BF16),
        compiler_params=_cparams("arbitrary"),
        name="attn_combine",
    )(*outs, *lses)


def _to_strided_order(x, dilation):
    s, c = x.shape
    return x.reshape(s // dilation, dilation, c).transpose(1, 0, 2).reshape(s, c)


def _from_strided_order(x, dilation):
    s, c = x.shape
    return x.reshape(dilation, s // dilation, c).transpose(1, 0, 2).reshape(s, c)


def _dilated_attention(h_bf, w_in_bf, bias):
    per_group = 3 * ATT_HEADS * ATT_HEAD_DIM
    outs, lses = [], []
    for g, (_, dilation) in enumerate(DILATION_CONFIGS):
        qkv = _matmul(_to_strided_order(h_bf, dilation), w_in_bf[:, g * per_group:(g + 1) * per_group], BF16)
        o, l = _dilated_branch(qkv, bias, g, dilation)
        outs.append(_from_strided_order(o, dilation))
        lses.append(_from_strided_order(l, dilation))
    return _attn_combine(outs, lses)


def _head_expand(n_heads, width):
    r = lax.broadcasted_iota(jnp.int32, (LANES, n_heads * width), 0)
    c = lax.broadcasted_iota(jnp.int32, (LANES, n_heads * width), 1)
    return jnp.where(_floor_to_pow2(c, width) == r * width, 1.0, 0.0).astype(BF16)


def _causal_conv_silu(cur, halo, w, b):
    rows = lax.broadcasted_iota(jnp.int32, (SUBLANES, cur.shape[1]), 0)
    acc = cur * w[SSM_CONV - 1:SSM_CONV, :] + b
    for j in range(1, SSM_CONV):
        shifted = pltpu.roll(cur, j, 0)
        head = jnp.where(rows < j, pltpu.roll(halo, j, 0), shifted[:SUBLANES])
        shifted = jnp.concatenate([head, shifted[SUBLANES:]], axis=0)
        acc = acc + shifted * w[SSM_CONV - 1 - j:SSM_CONV - j, :]
    return _silu(acc)


def _ssd_kernel(z_ref, x_ref, bc_ref, xh_ref, bch_ref, dt_ref, cwx_ref, cbx_ref, cwbc_ref, cbbc_ref,
                dtb_ref, aneg_ref, dskip_ref, nw_ref, o_ref, state_ref):
    c = pl.program_id(0)

    @pl.when(c == 0)
    def _():
        state_ref[...] = jnp.zeros_like(state_ref)

    first = c == 0
    xh = jnp.where(first, 0.0, xh_ref[...].astype(F32))
    bch = jnp.where(first, 0.0, bch_ref[...].astype(F32))
    xs = _causal_conv_silu(x_ref[...].astype(F32), xh, cwx_ref[...], cbx_ref[...])
    bc = _causal_conv_silu(bc_ref[...].astype(F32), bch, cwbc_ref[...], cbbc_ref[...])
    gn = SSM_GROUPS * SSM_STATE

    x_dt = dt_ref[...] + dtb_ref[...]
    dt = jnp.maximum(x_dt, 0.0) + jnp.log1p(jnp.exp(-jnp.abs(x_dt)))
    a = dt * aneg_ref[...]
    row = lax.broadcasted_iota(jnp.int32, (SSM_CHUNK, SSM_CHUNK), 0)
    colv = lax.broadcasted_iota(jnp.int32, (SSM_CHUNK, SSM_CHUNK), 1)
    causal = colv <= row
    tri = jnp.where(causal, 1.0, 0.0).astype(BF16)
    a_cs = _dot_01_left(jnp.concatenate([tri] * 3, axis=1), a)
    a_cs_t = a_cs.T
    expand = _head_expand(SSM_HEADS, SSM_HEAD_DIM)
    a_last = a_cs[SSM_CHUNK - 1:SSM_CHUNK, :]
    per_head = jnp.concatenate([dt, jnp.exp(a_cs), jnp.exp(a_last - a_cs)], axis=0)
    per_head_e = _dot_01_right(per_head, jnp.concatenate([expand] * 3, axis=0))
    dt_e = per_head_e[:SSM_CHUNK]
    in_decay_e = per_head_e[SSM_CHUNK:2 * SSM_CHUNK]
    out_decay_e = per_head_e[2 * SSM_CHUNK:]
    chunk_decay_e = in_decay_e[SSM_CHUNK - 1:SSM_CHUNK, :]

    xdt = xs * dt_e
    xdt_bf = xdt.astype(BF16)
    xend_bf = (xdt * out_decay_e).astype(BF16)
    gw = SSM_HEADS_PER_GROUP * SSM_HEAD_DIM
    ys = []
    for g in range(SSM_GROUPS):
        b_g = bc[:, g * SSM_STATE:(g + 1) * SSM_STATE]
        c_g = bc[:, gn + g * SSM_STATE:gn + (g + 1) * SSM_STATE].astype(BF16)
        cb = lax.dot_general(c_g, b_g.astype(BF16), (((1,), (1,)), ((), ())), preferred_element_type=F32)
        prev = state_ref[g]
        y_off = jnp.dot(c_g, prev.astype(BF16), preferred_element_type=F32) * in_decay_e[:, g * gw:(g + 1) * gw]
        y_parts = []
        for r in range(SSM_HEADS_PER_GROUP):
            hh = g * SSM_HEADS_PER_GROUP + r
            seg = jnp.exp(jnp.where(causal, a_cs[:, hh:hh + 1] - a_cs_t[hh:hh + 1, :], -jnp.inf))
            y_parts.append(jnp.dot((cb * seg).astype(BF16), xdt_bf[:, hh * SSM_HEAD_DIM:(hh + 1) * SSM_HEAD_DIM],
                                   preferred_element_type=F32))
        ys.append(jnp.concatenate(y_parts, axis=1) + y_off)
        new = jnp.dot(b_g.T.astype(BF16), xend_bf[:, g * gw:(g + 1) * gw], preferred_element_type=F32)
        state_ref[g] = prev * chunk_decay_e[:, g * gw:(g + 1) * gw] + new
    y = jnp.concatenate(ys, axis=1) + xs * dskip_ref[...]
    u = y * _silu(z_ref[...].astype(F32))
    gsz = SSM_D_INNER // SSM_GROUPS
    outs = []
    for g in range(SSM_GROUPS):
        ug = u[:, g * gsz:(g + 1) * gsz]
        outs.append(ug * lax.rsqrt(jnp.mean(ug * ug, axis=-1, keepdims=True) + NORM_EPS))
    o_ref[...] = (jnp.concatenate(outs, axis=1) * nw_ref[...]).astype(o_ref.dtype)


def _mamba2_inner(zxbc, dt_raw, conv_w, conv_b, dt_bias, a_log, d_skip, norm_w):
    s = zxbc.shape[0]
    n_c = s // SSM_CHUNK
    di = SSM_D_INNER
    halo_blocks = SSM_CHUNK // SUBLANES

    def pad_lanes(v):
        return jnp.pad(v.astype(F32), (0, LANES - v.shape[0])).reshape(1, LANES)

    cw = conv_w.astype(F32)
    cb = conv_b.astype(F32).reshape(1, -1)
    dskip = jnp.repeat(d_skip.astype(F32), SSM_HEAD_DIM).reshape(1, di)
    a_neg = pad_lanes(-jnp.exp(a_log.astype(F32)))

    def halo(colblk):
        return pl.BlockSpec((SUBLANES, di), lambda c: (jnp.maximum(c * halo_blocks - 1, 0), colblk))

    full = lambda shape: pl.BlockSpec(shape, lambda c: (0, 0))
    return pl.pallas_call(
        _ssd_kernel,
        grid=(n_c,),
        in_specs=[pl.BlockSpec((SSM_CHUNK, di), lambda c: (c, 0)),
                  pl.BlockSpec((SSM_CHUNK, di), lambda c: (c, 1)),
                  pl.BlockSpec((SSM_CHUNK, SSM_BC_DIM), lambda c: (c, 2)),
                  halo(1), halo(2),
                  pl.BlockSpec((SSM_CHUNK, LANES), lambda c: (c, 0)),
                  full((SSM_CONV, di)), full((1, di)), full((SSM_CONV, SSM_BC_DIM)), full((1, SSM_BC_DIM)),
                  full((1, LANES)), full((1, LANES)), full((1, di)), full((1, di))],
        out_specs=pl.BlockSpec((SSM_CHUNK, di), lambda c: (c, 0)),
        out_shape=jax.ShapeDtypeStruct((s, di), BF16),
        scratch_shapes=[pltpu.VMEM((SSM_GROUPS, SSM_STATE, SSM_HEADS_PER_GROUP * SSM_HEAD_DIM), F32)],
        compiler_params=_cparams("arbitrary"),
        name="ssd_chunk_scan",
    )(zxbc, zxbc, zxbc, zxbc, zxbc, dt_raw, cw[:, :di], cb[:, :di], cw[:, di:], cb[:, di:],
      pad_lanes(dt_bias), a_neg, dskip, norm_w.astype(F32).reshape(1, di))


def _mamba2_mixer(h_bf, w_in, conv_w, conv_b, dt_bias, a_log, d_skip, norm_w):
    n_main = SSM_D_INNER + SSM_D_INNER + SSM_BC_DIM
    assert SSM_BC_DIM == SSM_D_INNER
    w_main = w_in[:, :n_main].astype(BF16)
    w_dt = jnp.pad(w_in[:, n_main:], ((0, 0), (0, LANES - SSM_HEADS))).astype(BF16)
    zxbc = _matmul(h_bf, w_main, BF16)
    dt_raw = _matmul(h_bf, w_dt, F32)
    return _mamba2_inner(zxbc, dt_raw, conv_w, conv_b, dt_bias, a_log, d_skip, norm_w)


def _gla_kernel(q_ref, f_ref, v_ref, gate_ref, lb_ref, nw_ref, o_ref, state_ref):
    c = pl.program_id(0)

    @pl.when(c == 0)
    def _():
        state_ref[...] = jnp.zeros_like(state_ref)

    L = GLA_CHUNK
    K = HGRN_HEAD_DIM
    n_sub = L // GLA_SUB
    row = lax.broadcasted_iota(jnp.int32, (L, L), 0)
    col = lax.broadcasted_iota(jnp.int32, (L, L), 1)
    tri = jnp.where(col <= row, 1.0, 0.0).astype(BF16)
    tri3 = jnp.concatenate([tri] * 3, axis=1)
    sub3 = lax.broadcasted_iota(jnp.int32, (n_sub, GLA_SUB, L), 1)
    rel3 = (lax.broadcasted_iota(jnp.int32, (n_sub, GLA_SUB, L), 2)
            - lax.broadcasted_iota(jnp.int32, (n_sub, GLA_SUB, L), 0) * GLA_SUB)
    rel3 = jnp.where(rel3 >= 0, jnp.where(rel3 <= sub3, rel3, -1), -1)
    levels = []
    m = 2 * GLA_SUB
    while m <= L:
        levels.append(m)
        m *= 2
    same_block = {m: _floor_to_pow2(row, m) == _floor_to_pow2(col, m) for m in levels if m < L}

    for h in range(HGRN_HEADS):
        hs = slice(h * K, (h + 1) * K)
        f = f_ref[:, hs]
        lb = lb_ref[:, hs]
        q = _silu(q_ref[:, hs].astype(F32))
        v_bf = v_ref[:, hs]
        l1p = jnp.log1p(jnp.exp(-jnp.abs(f)))
        log_k = jnp.log1p(-lb) + jnp.minimum(-f, 0.0) - l1p
        ta = jnp.log(lb)
        tb = jnp.log1p(-lb) + jnp.minimum(f, 0.0) - l1p
        log_f = jnp.maximum(ta, tb) + jnp.log1p(jnp.exp(-jnp.abs(ta - tb)))
        g = _dot_01_left(tri3, log_f * LOG2_E)
        gk = g - log_k * LOG2_E

        g3 = g.reshape(n_sub, GLA_SUB, K)
        q3 = q.reshape(n_sub, GLA_SUB, K)
        gk3 = gk.reshape(n_sub, GLA_SUB, K)
        a3 = jnp.zeros((n_sub, GLA_SUB, L), F32)
        for s in range(GLA_SUB):
            e = jnp.exp2(g3 - gk3[:, s:s + 1, :])
            val = jnp.sum(q3 * e, axis=-1, keepdims=True)
            a3 = jnp.where(rel3 == s, val, a3)
        att = a3.reshape(L, L)

        for m in levels:
            half = m // 2
            zeros = jnp.zeros((half, K), F32)
            q_parts, k_parts = [], []
            for j in range(L // m):
                lo, mid, hi = j * m, j * m + half, (j + 1) * m
                gb = g[mid - 1:mid, :]
                q_parts += [zeros, q[mid:hi] * jnp.exp2(g[mid:hi] - gb)]
                k_parts += [jnp.exp2(gb - gk[lo:mid]), zeros]
            ql = jnp.concatenate(q_parts, axis=0).astype(BF16)
            kl = jnp.concatenate(k_parts, axis=0).astype(BF16)
            al = lax.dot_general(ql, kl, (((1,), (1,)), ((), ())), preferred_element_type=F32)
            att = att + (jnp.where(same_block[m], al, 0.0) if m < L else al)

        state_t = state_ref[h]
        o = (jnp.dot(att.astype(BF16), v_bf, preferred_element_type=F32)
             + lax.dot_general((q * jnp.exp2(g)).astype(BF16), state_t.astype(BF16), (((1,), (1,)), ((), ())),
                               preferred_element_type=F32))
        g_last = g[L - 1:L, :]
        k_dec = jnp.exp2(g_last - gk).astype(BF16)
        v_t = v_bf.astype(F32).T.astype(BF16)
        state_ref[h] = state_t * jnp.exp2(g_last) + jnp.dot(v_t, k_dec, preferred_element_type=F32)
        o = o * lax.rsqrt(jnp.mean(o * o, axis=-1, keepdims=True) + NORM_EPS) * nw_ref[...]
        o_ref[:, hs] = (o * _silu(gate_ref[:, hs].astype(F32))).astype(o_ref.dtype)


def _gla(qig, f, lb, norm_w):
    s, d = f.shape
    n_c = s // GLA_CHUNK
    return pl.pallas_call(
        _gla_kernel,
        grid=(n_c,),
        in_specs=[pl.BlockSpec((GLA_CHUNK, d), lambda c: (c, 0)),
                  pl.BlockSpec((GLA_CHUNK, d), lambda c: (c, 0)),
                  pl.BlockSpec((GLA_CHUNK, d), lambda c: (c, 1)),
                  pl.BlockSpec((GLA_CHUNK, d), lambda c: (c, 2)),
                  pl.BlockSpec((1, d), lambda c: (0, 0)),
                  pl.BlockSpec((1, HGRN_HEAD_DIM), lambda c: (0, 0))],
        out_specs=pl.BlockSpec((GLA_CHUNK, d), lambda c: (c, 0)),
        out_shape=jax.ShapeDtypeStruct((s, d), BF16),
        scratch_shapes=[pltpu.VMEM((HGRN_HEADS, HGRN_HEAD_DIM, HGRN_HEAD_DIM), F32)],
        compiler_params=_cparams("arbitrary"),
        name="gla_chunk_scan",
    )(qig, f, qig, qig, lb, norm_w)


def _hgrn2_mixer(h_bf, w_in, lb, norm_w):
    d = D_MODEL
    w_q, w_f, w_i, w_g = (w_in[:, j * d:(j + 1) * d] for j in range(4))
    qig = _matmul(h_bf, jnp.concatenate([w_q, w_i, w_g], axis=1).astype(BF16), BF16)
    f = _matmul(h_bf, w_f.astype(BF16), F32)
    return _gla(qig, f, lb.astype(F32).reshape(1, d), norm_w.astype(F32).reshape(1, HGRN_HEAD_DIM))


def kernel(x, attn_w_in, attn_w_out, rel_bias, ssm_w_in, ssm_conv_w, ssm_conv_b, ssm_dt_bias, ssm_a_log, ssm_d, ssm_norm_w, ssm_w_out, hgrn_w_in, hgrn_lower_bound, hgrn_norm_w, hgrn_w_out, moe_w_coarse, moe_w_fine, moe_w_gate, moe_w_up, moe_w_down, ln_gamma, ln_beta):
    b_, s_, d_ = x.shape
    assert b_ == 1 and d_ == D_MODEL
    lbs = jax.nn.softmax(hgrn_lower_bound.astype(F32), axis=0)
    lbs = jnp.cumsum(lbs, axis=0) - lbs[0]
    bias = _attention_bias(rel_bias)
    h = x.reshape(s_, d_).astype(F32)
    h_bf = h.astype(BF16)
    for layer in range(DEPTH):
        kind = layer % N_MIXERS
        j = layer // N_MIXERS
        if kind == 0:
            a = _dilated_attention(h_bf, attn_w_in[j].astype(BF16), bias)
            w_out = attn_w_out[j]
        elif kind == 1:
            a = _mamba2_mixer(h_bf, ssm_w_in[j], ssm_conv_w[j], ssm_conv_b[j], ssm_dt_bias[j], ssm_a_log[j],
                              ssm_d[j], ssm_norm_w[j])
            w_out = ssm_w_out[j]
        else:
            a = _hgrn2_mixer(h_bf, hgrn_w_in[j], lbs[layer], hgrn_norm_w[j])
            w_out = hgrn_w_out[j]
        gam = ln_gamma[layer].astype(F32)
        bet = ln_beta[layer].astype(F32)
        w_route = _router_weights(moe_w_coarse[layer], moe_w_fine[layer])
        h, route = _mix_ln_route(a, w_out.astype(BF16), h, gam[0:1], bet[0:1], w_route)
        h, h_bf = _moe_layer(h, route, moe_w_gate.astype(F32), moe_w_up.astype(F32), moe_w_down.astype(F32), layer,
                             gam[1:2], bet[1:2])
    return h.reshape(b_, s_, d_)
```
